```python
import math
import jax
import jax.numpy as jnp
from jax import lax
import numpy as np


D_MODEL = 1024
BATCH = 1
SEQ = 16384
DEPTH = 2

GRID_W = 64
CTX_LEN = 256
N_MIXERS = 2
N_GLA = (DEPTH + N_MIXERS - 1) // N_MIXERS
N_DIFF = DEPTH // N_MIXERS
GLA_HEADS = 4
GLA_DK = D_MODEL // 2 // GLA_HEADS
GLA_DV = D_MODEL // GLA_HEADS
GLA_GATE_RANK = 16
GLA_GATE_NORM = 16.0
GLA_CHUNK = 64
DIFF_HEAD_DIM = 64
DIFF_HEADS = D_MODEL // (2 * DIFF_HEAD_DIM)
ROPE_BASE = 10000.0
Q_BLOCK = 128
D_FF = 2816
CONV_WIDTH = 3
N_MOD = 6
EPS = 1e-6

kernel_name = 'hybrid_gla_diffattn_convffn_dit'


def rmsnorm(x, g):
    xf = x.astype(jnp.float32)
    y = xf * lax.rsqrt(jnp.mean(xf * xf, axis=-1, keepdims=True) + EPS)
    return y.astype(x.dtype) * g


def split_heads(t, heads):
    bsz, length, _ = t.shape
    return t.reshape(bsz, length, heads, -1).transpose(0, 2, 1, 3)


def merge_heads(t):
    bsz, heads, length, dim = t.shape
    return t.transpose(0, 2, 1, 3).reshape(bsz, length, heads * dim)


def axial_rope_tables(n_tokens, dtype):
    rows = n_tokens // GRID_W
    row = jnp.repeat(jnp.arange(rows, dtype=jnp.float32), GRID_W)
    col = jnp.tile(jnp.arange(GRID_W, dtype=jnp.float32), rows)
    quarter = DIFF_HEAD_DIM // 4
    inv = ROPE_BASE ** (-jnp.arange(quarter, dtype=jnp.float32) / quarter)
    ang_r = row[:, None] * inv
    ang_c = col[:, None] * inv
    ang = jnp.concatenate([ang_r, ang_r, ang_c, ang_c], axis=-1)
    return jnp.cos(ang).astype(dtype), jnp.sin(ang).astype(dtype)


def apply_axial_rope(x, cos, sin):
    xs = x.reshape(x.shape[:-1] + (2, 2, DIFF_HEAD_DIM // 4))
    rot = jnp.stack([-xs[..., 1, :], xs[..., 0, :]], axis=-2).reshape(x.shape)
    return x * cos + rot * sin


def gla_chunk_scan(q, k, v, log_g, s0):
    bsz, heads, length, _ = q.shape
    n_chunks = length // GLA_CHUNK

    def to_chunks(a):
        return a.reshape(bsz, heads, n_chunks, GLA_CHUNK, a.shape[-1]).transpose(2, 0, 1, 3, 4)

    mask = jnp.tril(jnp.ones((GLA_CHUNK, GLA_CHUNK), dtype=bool))[:, :, None]

    def step(state, inp):
        qb, kb, vb, gb = inp
        b = jnp.cumsum(gb, axis=2)
        rel = jnp.where(mask, b[:, :, :, None, :] - b[:, :, None, :, :], -jnp.inf)
        scores = jnp.einsum('bhtd,bhsd,bhtsd->bhts', qb, kb, jnp.exp(rel))
        out = (jnp.einsum('bhts,bhsv->bhtv', scores, vb)
               + jnp.einsum('bhtd,bhdv->bhtv', qb * jnp.exp(b), state))
        b_last = b[:, :, -1:, :]
        state = (jnp.exp(b_last[:, :, 0, :])[..., None] * state
                 + jnp.einsum('bhsd,bhsv->bhdv', kb * jnp.exp(b_last - b), vb))
        return state, out

    state, out = lax.scan(step, s0, (to_chunks(q), to_chunks(k), to_chunks(v), to_chunks(log_g)))
    out = out.transpose(1, 2, 0, 3, 4).reshape(bsz, heads, length, v.shape[-1])
    return out, state


def gla_mixer(hx, hc, wq, wk, wv, wr, wg1, wg2, bg, gn, wo, need_ctx):
    f32 = jnp.float32

    def project(h):
        q = split_heads(h @ wq, GLA_HEADS).astype(f32) * GLA_DK ** -0.5
        k = split_heads(h @ wk, GLA_HEADS).astype(f32)
        v = split_heads(h @ wv, GLA_HEADS).astype(f32)
        log_g = [split_heads(jax.nn.log_sigmoid(((h @ wg1[d]) @ wg2[d] + bg[d]).astype(f32))
                             / GLA_GATE_NORM, GLA_HEADS) for d in range(2)]
        return q, k, v, log_g

    def bidirectional(q, k, v, log_g, s_fwd, s_bwd):
        o_f, st_f = gla_chunk_scan(q, k, v, log_g[0], s_fwd)
        flip = lambda a: jnp.flip(a, axis=2)
        o_b, st_b = gla_chunk_scan(flip(q), flip(k), flip(v), flip(log_g[1]), s_bwd)
        return o_f + flip(o_b), st_f, st_b

    def readout(h, o):
        o = merge_heads(rmsnorm(o, gn)).astype(h.dtype) * jax.nn.silu(h @ wr)
        return o @ wo

    zero = jnp.zeros((hx.shape[0], GLA_HEADS, GLA_DK, GLA_DV), f32)
    qc, kc, vc, gc = project(hc)
    oc, ctx_fwd, ctx_bwd = bidirectional(qc, kc, vc, gc, zero, zero)
    qx, kx, vx, gx = project(hx)
    ox, _, _ = bidirectional(qx, kx, vx, gx, ctx_fwd, ctx_bwd)
    dx = readout(hx, ox)
    dc = readout(hc, oc) if need_ctx else None
    return dx, dc


def diff_mixer(hx, hc, wq, wk, wv, lq1, lk1, lq2, lk2, subln, wo, lambda_init, need_ctx):
    f32 = jnp.float32
    scale = DIFF_HEAD_DIM ** -0.5
    lam = (jnp.exp(jnp.sum(lq1 * lk1).astype(f32)) - jnp.exp(jnp.sum(lq2 * lk2).astype(f32))
           + lambda_init)

    def qkv(h):
        bsz, length, _ = h.shape
        q = (h @ wq).reshape(bsz, length, DIFF_HEADS, 2, DIFF_HEAD_DIM).transpose(3, 0, 2, 1, 4)
        k = (h @ wk).reshape(bsz, length, DIFF_HEADS, 2, DIFF_HEAD_DIM).transpose(3, 0, 2, 1, 4)
        v = split_heads(h @ wv, DIFF_HEADS)
        return q, k, v

    def diff_attend(q, k, v):
        s = jnp.einsum('nbhqd,nbhkd->nbhqk', q, k).astype(f32) * scale
        p = jax.nn.softmax(s, axis=-1)
        return jnp.einsum('bhqk,bhkv->bhqv', p[0] - lam * p[1], v)

    def readout(o, dtype):
        o = rmsnorm(o, subln) * (1.0 - lambda_init)
        return merge_heads(o).astype(dtype) @ wo

    qc, kc, vc = qkv(hc)
    qx, kx, vx = qkv(hx)
    bsz, n_lat = hx.shape[0], hx.shape[1]
    cos, sin = axial_rope_tables(n_lat, qx.dtype)
    qx = apply_axial_rope(qx, cos, sin)
    kx = apply_axial_rope(kx, cos, sin)
    k_all = jnp.concatenate([kc, kx], axis=3)
    v_all = jnp.concatenate([vc, vx], axis=2).astype(f32)
    n_blocks = n_lat // Q_BLOCK
    q_blocks = qx.reshape(2, bsz, DIFF_HEADS, n_blocks, Q_BLOCK, DIFF_HEAD_DIM).transpose(3, 0, 1, 2, 4, 5)
    o_blocks = lax.map(lambda qb: diff_attend(qb, k_all, v_all), q_blocks)
    ox = o_blocks.transpose(1, 2, 0, 3, 4).reshape(bsz, DIFF_HEADS, n_lat, 2 * DIFF_HEAD_DIM)
    dx = readout(ox, hx.dtype)
    dc = readout(diff_attend(qc, kc, vc.astype(f32)), hc.dtype) if need_ctx else None
    return dx, dc


def conv_ffn(h, w_up, w_conv, b_conv, w_down):
    u = h @ w_up
    up = jnp.pad(u, ((0, 0), (1, 1), (0, 0)))
    u = up[:, :-2] * w_conv[0] + up[:, 1:-1] * w_conv[1] + up[:, 2:] * w_conv[2] + b_conv
    a, b = jnp.split(u, 2, axis=-1)
    return (jax.nn.silu(a) * b) @ w_down


def setup_inputs(seed: int = 0) -> dict:
    key = jax.random.key(seed)
    ks = iter(jax.random.split(key, 40))
    f32 = jnp.float32

    def nrm(shape, fan_in):
        return jax.random.normal(next(ks), shape, f32) * fan_in ** -0.5

    def gain(shape):
        return 1.0 + 0.02 * jax.random.normal(next(ks), shape, f32)

    def small(shape, s):
        return s * jax.random.normal(next(ks), shape, f32)

    D = D_MODEL
    return {
        'x': jax.random.normal(next(ks), (BATCH, SEQ, D), f32),
        'c': jax.random.normal(next(ks), (BATCH, D), f32),
        'ctx': jax.random.normal(next(ks), (BATCH, CTX_LEN, D), f32),
        'c_ctx': jax.random.normal(next(ks), (D,), f32),
        'mod_w': nrm((DEPTH, D, N_MOD * D), D),
        'mod_b': small((DEPTH, N_MOD * D), 0.01),
        'norm_mix': gain((DEPTH, D)),
        'norm_ffn': gain((DEPTH, D)),
        'gla_wq': nrm((N_GLA, D, GLA_HEADS * GLA_DK), D),
        'gla_wk': nrm((N_GLA, D, GLA_HEADS * GLA_DK), D),
        'gla_wv': nrm((N_GLA, D, GLA_HEADS * GLA_DV), D),
        'gla_wr': nrm((N_GLA, D, GLA_HEADS * GLA_DV), D),
        'gla_wg1': nrm((N_GLA, 2, D, GLA_GATE_RANK), D),
        'gla_wg2': nrm((N_GLA, 2, GLA_GATE_RANK, GLA_HEADS * GLA_DK), GLA_GATE_RANK),
        'gla_bg': small((N_GLA, 2, GLA_HEADS * GLA_DK), 0.1),
        'gla_norm': gain((N_GLA, GLA_DV)),
        'gla_wo': nrm((N_GLA, GLA_HEADS * GLA_DV, D), GLA_HEADS * GLA_DV),
        'diff_wq': nrm((N_DIFF, D, 2 * DIFF_HEADS * DIFF_HEAD_DIM), D),
        'diff_wk': nrm((N_DIFF, D, 2 * DIFF_HEADS * DIFF_HEAD_DIM), D),
        'diff_wv': nrm((N_DIFF, D, DIFF_HEADS * 2 * DIFF_HEAD_DIM), D),
        'diff_lq1': small((N_DIFF, DIFF_HEAD_DIM), 0.1),
        'diff_lk1': small((N_DIFF, DIFF_HEAD_DIM), 0.1),
        'diff_lq2': small((N_DIFF, DIFF_HEAD_DIM), 0.1),
        'diff_lk2': small((N_DIFF, DIFF_HEAD_DIM), 0.1),
        'diff_subln': gain((N_DIFF, 2 * DIFF_HEAD_DIM)),
        'diff_wo': nrm((N_DIFF, DIFF_HEADS * 2 * DIFF_HEAD_DIM, D), DIFF_HEADS * 2 * DIFF_HEAD_DIM),
        'ffn_wup': nrm((DEPTH, D, 2 * D_FF), D),
        'ffn_conv': nrm((DEPTH, CONV_WIDTH, 2 * D_FF), CONV_WIDTH),
        'ffn_conv_b': small((DEPTH, 2 * D_FF), 0.01),
        'ffn_wdown': nrm((DEPTH, D_FF, D), D_FF),
        'final_norm': gain((D,)),
    }


def reference(x, c, ctx, c_ctx, mod_w, mod_b, norm_mix, norm_ffn,
              gla_wq, gla_wk, gla_wv, gla_wr, gla_wg1, gla_wg2, gla_bg, gla_norm, gla_wo,
              diff_wq, diff_wk, diff_wv, diff_lq1, diff_lk1, diff_lq2, diff_lk2, diff_subln, diff_wo,
              ffn_wup, ffn_conv, ffn_conv_b, ffn_wdown, final_norm):
    silu_c = jax.nn.silu(c)[:, None, :]
    silu_cc = jax.nn.silu(c_ctx)
    for i in range(DEPTH):
        last = i == DEPTH - 1
        mx = jnp.split(silu_c @ mod_w[i] + mod_b[i], N_MOD, axis=-1)
        mc = jnp.split(silu_cc @ mod_w[i] + mod_b[i], N_MOD, axis=-1)
        hx = rmsnorm(x, norm_mix[i]) * (1 + mx[1]) + mx[0]
        hc = rmsnorm(ctx, norm_mix[i]) * (1 + mc[1]) + mc[0]
        j = i // N_MIXERS
        if i % N_MIXERS == 0:
            dx, dc = gla_mixer(hx, hc, gla_wq[j], gla_wk[j], gla_wv[j], gla_wr[j], gla_wg1[j],
                               gla_wg2[j], gla_bg[j], gla_norm[j], gla_wo[j], not last)
        else:
            lambda_init = 0.8 - 0.6 * math.exp(-0.3 * i)
            dx, dc = diff_mixer(hx, hc, diff_wq[j], diff_wk[j], diff_wv[j], diff_lq1[j], diff_lk1[j],
                                diff_lq2[j], diff_lk2[j], diff_subln[j], diff_wo[j], lambda_init, not last)
        x = x + mx[2] * dx
        x = x + mx[5] * conv_ffn(rmsnorm(x, norm_ffn[i]) * (1 + mx[4]) + mx[3],
                                 ffn_wup[i], ffn_conv[i], ffn_conv_b[i], ffn_wdown[i])
        if not last:
            ctx = ctx + mc[2] * dc
            ctx = ctx + mc[5] * conv_ffn(rmsnorm(ctx, norm_ffn[i]) * (1 + mc[4]) + mc[3],
                                         ffn_wup[i], ffn_conv[i], ffn_conv_b[i], ffn_wdown[i])
    return rmsnorm(x, final_norm)
```

```python
import functools
import math

import numpy as np
import jax
import jax.numpy as jnp
from jax import lax
from jax.experimental import pallas as pl
from jax.experimental.pallas import tpu as pltpu

GRID_W = 64
GLA_HEADS = 4
GLA_GATE_RANK = 16
GLA_GATE_NORM = 16.0
GLA_CHUNK = 64
DIFF_HEAD_DIM = 64
ROPE_BASE = 10000.0
CONV_WIDTH = 3
N_MOD = 6
EPS = 1e-6

LANES = 128
BF16_SUBLANES = 16
VMEM_LIMIT = 56 * 1024 * 1024

ROW_TILE = 256
FFN_COL_CHUNK = 256
ATT_TQ = 512
ATT_TK = 640
MOD_COL_TILE = 1536

F32 = jnp.float32
BF16 = jnp.bfloat16

_NT = (((1,), (1,)), ((), ()))
_TN = (((0,), (0,)), ((), ()))


def _cparams(*sem):
    return pltpu.CompilerParams(dimension_semantics=sem, vmem_limit_bytes=VMEM_LIMIT)


def _const_spec(shape):
    nd = len(shape)
    return pl.BlockSpec(shape, lambda *_: (0,) * nd, pipeline_mode=pl.Buffered(1))


def _silu(x):
    return x * (1.0 / (1.0 + jnp.exp(-x)))


def _rms(x):
    return x * lax.rsqrt(jnp.mean(x * x, axis=-1, keepdims=True) + EPS)


def _mod_rows(modv_ref, is_ctx, k, d):
    sl = slice(k * d, (k + 1) * d)
    return jnp.where(is_ctx, modv_ref[1:2, sl], modv_ref[0:1, sl])


def _modnorm(x, g, scale, shift):
    return (_rms(x) * g) * (1.0 + scale) + shift


def _split2(x):
    hi = x.astype(BF16)
    lo = (x - hi.astype(F32)).astype(BF16)
    return hi, lo


def _mod_kernel(cc_ref, w_ref, b_ref, o_ref):
    s = _silu(cc_ref[...])
    w = w_ref[0]
    s_hi, s_lo = _split2(s)
    w_hi, w_lo = _split2(w)
    acc = jnp.dot(s_hi, w_hi, preferred_element_type=F32)
    acc += jnp.dot(s_lo, w_hi, preferred_element_type=F32)
    acc += jnp.dot(s_hi, w_lo, preferred_element_type=F32)
    o_ref[0] = acc + b_ref[0]


def _modulation(cc, mod_w, mod_b):
    depth, d, n = mod_w.shape
    tn = MOD_COL_TILE
    return pl.pallas_call(
        _mod_kernel,
        grid=(depth, n // tn),
        in_specs=[pl.BlockSpec((8, d), lambda l, j: (0, 0)),
                  pl.BlockSpec((1, d, tn), lambda l, j: (l, 0, j)),
                  pl.BlockSpec((1, 1, tn), lambda l, j: (l, 0, j))],
        out_specs=pl.BlockSpec((1, 8, tn), lambda l, j: (l, 0, j)),
        out_shape=jax.ShapeDtypeStruct((depth, 8, n), F32),
        compiler_params=_cparams("parallel", "parallel"),
        name="modulation",
    )(cc, mod_w, mod_b.reshape(depth, 1, n))


def _gla_proj_kernel(ctx_ref, x_ref, modv_ref, g_ref, w_ref, wg1_ref, wg2_ref, bg_ref,
                     q_ref, k_ref, v_ref, r_ref, lg_ref, *, dk_total, dv_total, q_scale):
    d = x_ref.shape[1]
    is_ctx = pl.program_id(0) == 0
    xt = jnp.where(is_ctx, ctx_ref[...], x_ref[...])
    h = _modnorm(xt, g_ref[...], _mod_rows(modv_ref, is_ctx, 1, d),
                 _mod_rows(modv_ref, is_ctx, 0, d)).astype(BF16)
    qkvr = jnp.dot(h, w_ref[...], preferred_element_type=F32)
    o1 = dk_total
    o2 = 2 * dk_total
    o3 = o2 + dv_total
    q_ref[...] = qkvr[:, :o1] * q_scale
    k_ref[...] = qkvr[:, o1:o2]
    v_ref[...] = qkvr[:, o2:o3].astype(BF16)
    r_ref[...] = _silu(qkvr[:, o3:]).astype(BF16)
    low = jnp.dot(h, wg1_ref[...], preferred_element_type=F32).astype(BF16)
    z = jnp.dot(low, wg2_ref[...], preferred_element_type=F32) + bg_ref[...]
    log_sig = jnp.minimum(z, 0.0) - jnp.log1p(jnp.exp(-jnp.abs(z)))
    lg_ref[...] = log_sig * (1.0 / GLA_GATE_NORM)


def _gla_proj(ctx2, x2, modv, norm_g, wq, wk, wv, wr, wg1, wg2, bg):
    n_ctx, d = ctx2.shape
    n_lat = x2.shape[0]
    t = ROW_TILE
    assert n_ctx == t and n_lat % t == 0
    n_all = n_ctx + n_lat
    dk_total = wq.shape[1]
    dv_total = wv.shape[1]
    rank = wg1.shape[2]
    w = jnp.concatenate([wq, wk, wv, wr], axis=1).astype(BF16)
    wg1c = jnp.zeros((d, LANES), F32).at[:, :rank].set(wg1[0]).at[:, rank:2 * rank].set(wg1[1])
    wg2c = (jnp.zeros((LANES, 2 * dk_total), F32)
            .at[:rank, :dk_total].set(wg2[0]).at[rank:2 * rank, dk_total:].set(wg2[1]))
    bgc = bg.reshape(1, 2 * dk_total)
    row = lambda i: (i, 0)
    kern = functools.partial(_gla_proj_kernel, dk_total=dk_total, dv_total=dv_total,
                             q_scale=(dk_total // GLA_HEADS) ** -0.5)
    return pl.pallas_call(
        kern,
        grid=(n_all // t,),
        in_specs=[pl.BlockSpec((t, d), lambda i: (0, 0)),
                  pl.BlockSpec((t, d), lambda i: (jnp.maximum(i - 1, 0), 0)),
                  _const_spec(modv.shape), _const_spec((1, d)),
                  _const_spec(w.shape), _const_spec(wg1c.shape), _const_spec(wg2c.shape),
                  _const_spec(bgc.shape)],
        out_specs=[pl.BlockSpec((t, dk_total), row), pl.BlockSpec((t, dk_total), row),
                   pl.BlockSpec((t, dv_total), row), pl.BlockSpec((t, dv_total), row),
                   pl.BlockSpec((t, 2 * dk_total), row)],
        out_shape=[jax.ShapeDtypeStruct((n_all, dk_total), F32),
                   jax.ShapeDtypeStruct((n_all, dk_total), F32),
                   jax.ShapeDtypeStruct((n_all, dv_total), BF16),
                   jax.ShapeDtypeStruct((n_all, dv_total), BF16),
                   jax.ShapeDtypeStruct((n_all, 2 * dk_total), F32)],
        compiler_params=_cparams("parallel"),
        name="gla_proj",
    )(ctx2, x2, modv, norm_g.reshape(1, d), w, wg1c.astype(BF16), wg2c.astype(BF16), bgc)


N_LEVELS = int(math.log2(GLA_CHUNK))
N_DECAY_BLOCKS = 2 + 2 * N_LEVELS


def _scan_tables():
    c = GLA_CHUNK
    t = np.arange(c)[:, None]
    u = np.arange(c)[None, :]
    blocks = [(u <= t), (u > t)]
    dq, dk, masks = [], [], [(t == u)]
    for lvl in range(1, N_LEVELS + 1):
        size = 1 << lvl
        half = size // 2
        start = (t // size) * size
        mid = start + half - 1
        upper = (t - start) >= half
        dq.append(upper & (u > mid) & (u <= t))
        dk.append((~upper) & (u > t) & (u <= mid))
        same = (t // size) == (u // size)
        masks.append(same & upper & ((u - (u // size) * size) < half))
    fwd = np.concatenate(blocks + dq + dk, axis=0).astype(np.float32)
    fmask = np.concatenate(masks, axis=0).astype(np.float32)
    rev = lambda m: m.reshape(-1, c, c)[:, ::-1, ::-1].reshape(-1, c)
    return (np.stack([fwd, rev(fwd)]), np.stack([fmask, rev(fmask)]))


def _scan_direction(q_ref, k_ref, v_ref, g_ref, dmat, mask_ref, direction, st_ref, o_ref):
    c = GLA_CHUNK
    dk = q_ref.shape[1] // GLA_HEADS
    dv = v_ref.shape[1] // GLA_HEADS
    g_hi, g_lo = _split2(g_ref[...])
    e = (jnp.dot(dmat, g_hi, preferred_element_type=F32)
         + jnp.dot(dmat, g_lo, preferred_element_type=F32))
    x = jnp.exp(e)
    last = c - 1 if direction == 0 else 0
    q_all = q_ref[...]
    k_all = k_ref[...]
    v_all = v_ref[...]
    blk = lambda n, hs: x[n * c:(n + 1) * c, hs]
    for h in range(GLA_HEADS):
        hs = slice(h * dk, (h + 1) * dk)
        q = q_all[:, hs]
        k = k_all[:, hs]
        v = v_all[:, h * dv:(h + 1) * dv]
        a = mask_ref[direction, 0:c, :] * lax.dot_general(
            q.astype(BF16), k.astype(BF16), _NT, preferred_element_type=F32)
        for lvl in range(1, N_LEVELS + 1):
            ql = (q * blk(1 + lvl, hs)).astype(BF16)
            kl = (k * blk(1 + N_LEVELS + lvl, hs)).astype(BF16)
            a += mask_ref[direction, lvl * c:(lvl + 1) * c, :] * lax.dot_general(
                ql, kl, _NT, preferred_element_type=F32)
        st = st_ref[direction, h]
        o = jnp.dot(a.astype(BF16), v, preferred_element_type=F32)
        o += lax.dot_general((q * blk(0, hs)).astype(BF16), st.astype(BF16), _NT,
                             preferred_element_type=F32)
        o_ref[:, h * dv:(h + 1) * dv] = o
        k_end = (k * blk(1, hs)).astype(BF16)
        decay = x[last:last + 1, hs]
        st_ref[direction, h] = st * decay + lax.dot_general(
            v, k_end, _TN, preferred_element_type=F32)


def _gla_scan_kernel(dmat_ref, mask_ref, qf, kf, vf, gf, qb, kb, vb, gb, of_ref, ob_ref, st_ref):
    @pl.when(pl.program_id(0) == 0)
    def _():
        st_ref[...] = jnp.zeros_like(st_ref)

    _scan_direction(qf, kf, vf, gf, dmat_ref[0], mask_ref, 0, st_ref, of_ref)
    _scan_direction(qb, kb, vb, gb, dmat_ref[1], mask_ref, 1, st_ref, ob_ref)


def _gla_scan(q, k, v, lg, n_ctx):
    n_all, dk_total = q.shape
    dv_total = v.shape[1]
    c = GLA_CHUNK
    n_chunks = n_all // c
    ctx_chunks = n_ctx // c
    dmat, masks = _scan_tables()
    fwd = lambda i: (i, 0)
    bidx = lambda i: jnp.where(i < ctx_chunks, ctx_chunks - 1 - i, n_chunks + ctx_chunks - 1 - i)
    bwd = lambda i: (bidx(i), 0)
    bwd_g = lambda i: (bidx(i), 1)
    return pl.pallas_call(
        _gla_scan_kernel,
        grid=(n_chunks,),
        in_specs=[_const_spec(dmat.shape), _const_spec(masks.shape),
                  pl.BlockSpec((c, dk_total), fwd), pl.BlockSpec((c, dk_total), fwd),
                  pl.BlockSpec((c, dv_total), fwd), pl.BlockSpec((c, dk_total), fwd),
                  pl.BlockSpec((c, dk_total), bwd), pl.BlockSpec((c, dk_total), bwd),
                  pl.BlockSpec((c, dv_total), bwd), pl.BlockSpec((c, dk_total), bwd_g)],
        out_specs=[pl.BlockSpec((c, dv_total), fwd), pl.BlockSpec((c, dv_total), bwd)],
        out_shape=[jax.ShapeDtypeStruct((n_all, dv_total), F32),
                   jax.ShapeDtypeStruct((n_all, dv_total), F32)],
        scratch_shapes=[pltpu.VMEM((2, GLA_HEADS, dv_total // GLA_HEADS, dk_total // GLA_HEADS), F32)],
        compiler_params=_cparams("arbitrary"),
        name="gla_scan",
    )(jnp.asarray(dmat, BF16), jnp.asarray(masks, F32), q, k, v, lg, q, k, v, lg)


def _gla_readout_kernel(ctx_ref, x_ref, of_ref, ob_ref, r_ref, modv_ref, gn_ref, wo_ref, out_ref):
    d = x_ref.shape[1]
    dv = gn_ref.shape[1]
    is_ctx = pl.program_id(0) == 0
    xt = jnp.where(is_ctx, ctx_ref[...], x_ref[...])
    o = of_ref[...] + ob_ref[...]
    parts = []
    for h in range(GLA_HEADS):
        parts.append(_rms(o[:, h * dv:(h + 1) * dv]) * gn_ref[...])
    y = (jnp.concatenate(parts, axis=1) * r_ref[...].astype(F32)).astype(BF16)
    dx = jnp.dot(y, wo_ref[...], preferred_element_type=F32)
    out_ref[...] = xt + _mod_rows(modv_ref, is_ctx, 2, d) * dx


def _gla_readout(ctx2, x2, o_f, o_b, gr, modv, gn, wo):
    n_ctx, d = ctx2.shape
    n_all, dv_total = o_f.shape
    t = ROW_TILE
    row = lambda i: (i, 0)
    return pl.pallas_call(
        _gla_readout_kernel,
        grid=(n_all // t,),
        in_specs=[pl.BlockSpec((t, d), lambda i: (0, 0)),
                  pl.BlockSpec((t, d), lambda i: (jnp.maximum(i - 1, 0), 0)),
                  pl.BlockSpec((t, dv_total), row), pl.BlockSpec((t, dv_total), row),
                  pl.BlockSpec((t, dv_total), row),
                  _const_spec(modv.shape), _const_spec((1, gn.shape[0])), _const_spec(wo.shape)],
        out_specs=pl.BlockSpec((t, d), row),
        out_shape=jax.ShapeDtypeStruct((n_all, d), F32),
        compiler_params=_cparams("parallel"),
        name="gla_readout",
    )(ctx2, x2, o_f, o_b, gr, modv, gn.reshape(1, -1), wo.astype(BF16))


HALO = BF16_SUBLANES


def _ffn_kernel(prev_ref, x_ref, next_ref, modv_ref, g_ref, wup_ref, cw_ref, cb_ref, wdn_ref,
                fin_ref, out_ref, lhs_ref, ua_ref, ub_ref, acc_ref, *, ctx_tiles, final):
    t, d = x_ref.shape
    d_ff = wdn_ref.shape[0]
    i = pl.program_id(0)
    last = pl.num_programs(0) - 1
    is_ctx = i < ctx_tiles
    has_prev = jnp.logical_and(i != 0, i != ctx_tiles)
    has_next = jnp.logical_and(i != last, i != ctx_tiles - 1)
    g = g_ref[...]
    scale = _mod_rows(modv_ref, is_ctx, 4, d)
    shift = _mod_rows(modv_ref, is_ctx, 3, d)
    xt = x_ref[...]
    lhs_ref[0:HALO, :] = jnp.where(has_prev, _modnorm(prev_ref[...], g, scale, shift), 0.0).astype(BF16)
    lhs_ref[HALO:HALO + t, :] = _modnorm(xt, g, scale, shift).astype(BF16)
    lhs_ref[HALO + t:, :] = jnp.where(has_next, _modnorm(next_ref[...], g, scale, shift), 0.0).astype(BF16)
    lhs = lhs_ref[...]
    cc = FFN_COL_CHUNK
    for c in range(d_ff // cc):
        halves = []
        for u_ref, off in ((ua_ref, c * cc), (ub_ref, d_ff + c * cc)):
            cols = slice(off, off + cc)
            u_ref[...] = jnp.dot(lhs, wup_ref[:, cols], preferred_element_type=F32)
            halves.append(u_ref[HALO - 1:HALO - 1 + t, :] * cw_ref[0:1, cols]
                          + u_ref[HALO:HALO + t, :] * cw_ref[1:2, cols]
                          + u_ref[HALO + 1:HALO + 1 + t, :] * cw_ref[2:3, cols]
                          + cb_ref[:, cols])
        act = (_silu(halves[0]) * halves[1]).astype(BF16)
        part = jnp.dot(act, wdn_ref[c * cc:(c + 1) * cc, :], preferred_element_type=F32)
        if c == 0:
            acc_ref[...] = part
        else:
            acc_ref[...] += part
    y = xt + _mod_rows(modv_ref, is_ctx, 5, d) * acc_ref[...]
    if final:
        y = _rms(y) * fin_ref[...]
    out_ref[...] = y


def _conv_ffn(xs, modv, norm_g, w_up, w_conv, b_conv, w_down, final_g, *, ctx_tiles, final):
    n, d = xs.shape
    d_ff = w_down.shape[0]
    t = ROW_TILE
    assert d_ff % FFN_COL_CHUNK == 0
    per = t // HALO
    n_halo_blocks = n // HALO
    kern = functools.partial(_ffn_kernel, ctx_tiles=ctx_tiles, final=final)
    return pl.pallas_call(
        kern,
        grid=(n // t,),
        in_specs=[pl.BlockSpec((HALO, d), lambda i: (jnp.maximum(i * per - 1, 0), 0)),
                  pl.BlockSpec((t, d), lambda i: (i, 0)),
                  pl.BlockSpec((HALO, d), lambda i: (jnp.minimum((i + 1) * per, n_halo_blocks - 1), 0)),
                  _const_spec(modv.shape), _const_spec((1, d)),
                  _const_spec(w_up.shape), _const_spec(w_conv.shape), _const_spec((1, 2 * d_ff)),
                  _const_spec(w_down.shape), _const_spec((1, d))],
        out_specs=pl.BlockSpec((t, d), lambda i: (i, 0)),
        out_shape=jax.ShapeDtypeStruct((n, d), F32),
        scratch_shapes=[pltpu.VMEM((t + 2 * HALO, d), BF16),
                        pltpu.VMEM((t + 2 * HALO, FFN_COL_CHUNK), F32),
                        pltpu.VMEM((t + 2 * HALO, FFN_COL_CHUNK), F32),
                        pltpu.VMEM((t, d), F32)],
        compiler_params=_cparams("parallel"),
        name="conv_ffn_final" if final else "conv_ffn",
    )(xs, xs, xs, modv, norm_g.reshape(1, d), w_up.astype(BF16), w_conv,
      b_conv.reshape(1, -1), w_down.astype(BF16), final_g.reshape(1, d))


def _rope_tables(n_tokens, width):
    rows = n_tokens // GRID_W
    row = jnp.repeat(jnp.arange(rows, dtype=F32), GRID_W)
    col = jnp.tile(jnp.arange(GRID_W, dtype=F32), rows)
    quarter = DIFF_HEAD_DIM // 4
    inv = ROPE_BASE ** (-jnp.arange(quarter, dtype=F32) / quarter)
    ang_r = row[:, None] * inv
    ang_c = col[:, None] * inv
    ang = jnp.concatenate([ang_r, ang_r, ang_c, ang_c], axis=-1)
    sign = jnp.tile(jnp.concatenate([-jnp.ones((quarter,), F32), jnp.ones((quarter,), F32)]), 2)
    reps = width // DIFF_HEAD_DIM
    return jnp.tile(jnp.cos(ang), (1, reps)), jnp.tile(jnp.sin(ang) * sign, (1, reps))


def _diff_proj_kernel(x_ref, cos_ref, sin_ref, modv_ref, g_ref, w_ref, qt_ref, k_ref, vt_ref):
    t, d = x_ref.shape
    is_ctx = pl.program_id(0) == 0
    h = _modnorm(x_ref[...], g_ref[...], _mod_rows(modv_ref, is_ctx, 1, d),
                 _mod_rows(modv_ref, is_ctx, 0, d)).astype(BF16)
    qkv = jnp.dot(h, w_ref[...], preferred_element_type=F32)
    n = qkv.shape[1] // 3
    quarter = DIFF_HEAD_DIM // 4
    reps = n // LANES
    cos = jnp.concatenate([cos_ref[...]] * reps, axis=1)
    sin = jnp.concatenate([sin_ref[...]] * reps, axis=1)
    lane = lax.broadcasted_iota(jnp.int32, (t, n), 1)
    first_half = (lane % (2 * quarter)) < quarter

    def rope(a):
        partner = jnp.where(first_half, pltpu.roll(a, n - quarter, 1), pltpu.roll(a, quarter, 1))
        return jnp.where(is_ctx, a, a * cos + partner * sin)

    q = rope(qkv[:, :n]) * (DIFF_HEAD_DIM ** -0.5)
    qt_ref[...] = q.T.astype(BF16)
    k_ref[...] = rope(qkv[:, n:2 * n]).astype(BF16)
    vt_ref[...] = qkv[:, 2 * n:].T.astype(BF16)


def _diff_proj(xs, n_ctx, modv, norm_g, wq, wk, wv):
    n_all, d = xs.shape
    n_lat = n_all - n_ctx
    t = ROW_TILE
    n = wq.shape[1]
    w = jnp.concatenate([wq, wk, wv], axis=1).astype(BF16)
    cos, sin = _rope_tables(n_lat, LANES)
    lat = lambda i: (jnp.maximum(i - 1, 0), 0)
    return pl.pallas_call(
        _diff_proj_kernel,
        grid=(n_all // t,),
        in_specs=[pl.BlockSpec((t, d), lambda i: (i, 0)),
                  pl.BlockSpec((t, LANES), lat), pl.BlockSpec((t, LANES), lat),
                  _const_spec(modv.shape), _const_spec((1, d)), _const_spec(w.shape)],
        out_specs=[pl.BlockSpec((n, t), lambda i: (0, jnp.maximum(i - 1, 0))),
                   pl.BlockSpec((t, n), lambda i: (i, 0)),
                   pl.BlockSpec((n, t), lambda i: (0, i))],
        out_shape=[jax.ShapeDtypeStruct((n, n_lat), BF16),
                   jax.ShapeDtypeStruct((n_all, n), BF16),
                   jax.ShapeDtypeStruct((n, n_all), BF16)],
        compiler_params=_cparams("arbitrary"),
        name="diff_proj",
    )(xs, cos, sin, modv, norm_g.reshape(1, d), w)


def _diff_attn_kernel(lam_ref, qt_ref, k_ref, vt_ref, o_ref, m_ref, l_ref, acc_ref, *, lambda_init):
    hd = DIFF_HEAD_DIM
    tq = qt_ref.shape[1]
    n_keys = k_ref.shape[0]
    tk = ATT_TK
    qt = qt_ref[...]
    rows = lax.broadcasted_iota(jnp.int32, qt.shape, 0)
    zero = jnp.zeros_like(qt)
    q_br = (jnp.where(rows < hd, qt, zero), jnp.where(rows >= hd, qt, zero))
    m_ref[...] = jnp.full_like(m_ref, -jnp.inf)
    l_ref[...] = jnp.zeros_like(l_ref)
    acc_ref[...] = jnp.zeros_like(acc_ref)

    def body(j, carry):
        start = pl.multiple_of(j * tk, LANES)
        kb = k_ref[pl.ds(start, tk), :]
        vb = vt_ref[:, pl.ds(start, tk)]
        for br in range(2):
            s = jnp.dot(kb, q_br[br], preferred_element_type=F32)
            m_old = m_ref[br]
            m_new = jnp.maximum(m_old, jnp.max(s, axis=0, keepdims=True))
            p = jnp.exp(s - m_new)
            alpha = jnp.exp(m_old - m_new)
            l_ref[br] = alpha * l_ref[br] + jnp.sum(p, axis=0, keepdims=True)
            acc_ref[br] = alpha * acc_ref[br] + jnp.dot(vb, p.astype(BF16),
                                                        preferred_element_type=F32)
            m_ref[br] = m_new
        return carry

    lax.fori_loop(0, n_keys // tk, body, 0)
    lam = (jnp.exp(jnp.sum(lam_ref[0:1, :] * lam_ref[1:2, :], axis=1, keepdims=True))
           - jnp.exp(jnp.sum(lam_ref[2:3, :] * lam_ref[3:4, :], axis=1, keepdims=True))
           + lambda_init)
    o_ref[...] = acc_ref[0] / l_ref[0] - lam * (acc_ref[1] / l_ref[1])


def _diff_attn(qt, k, vt, lam_vecs, lambda_init):
    n, n_lat = qt.shape
    n_all = k.shape[0]
    hw = 2 * DIFF_HEAD_DIM
    heads = n // hw
    tq = ATT_TQ
    assert n_lat % tq == 0 and n_all % ATT_TK == 0
    kern = functools.partial(_diff_attn_kernel, lambda_init=lambda_init)
    return pl.pallas_call(
        kern,
        grid=(heads, n_lat // tq),
        in_specs=[pl.BlockSpec(lam_vecs.shape, lambda h, i: (0, 0)),
                  pl.BlockSpec((hw, tq), lambda h, i: (h, i)),
                  pl.BlockSpec((n_all, hw), lambda h, i: (0, h)),
                  pl.BlockSpec((hw, n_all), lambda h, i: (h, 0))],
        out_specs=pl.BlockSpec((hw, tq), lambda h, i: (h, i)),
        out_shape=jax.ShapeDtypeStruct((n, n_lat), F32),
        scratch_shapes=[pltpu.VMEM((2, 1, tq), F32), pltpu.VMEM((2, 1, tq), F32),
                        pltpu.VMEM((2, hw, tq), F32)],
        compiler_params=_cparams("arbitrary", "arbitrary"),
        name="diff_attn",
    )(lam_vecs, qt, k, vt)


def _diff_readout_kernel(x_ref, ot_ref, modv_ref, sub_ref, wo_ref, out_ref, *, out_scale):
    d = x_ref.shape[1]
    hw = sub_ref.shape[0]
    ot = ot_ref[...]
    parts = []
    for h in range(ot.shape[0] // hw):
        oh = ot[h * hw:(h + 1) * hw, :]
        ms = jnp.mean(oh * oh, axis=0, keepdims=True)
        parts.append(oh * lax.rsqrt(ms + EPS) * sub_ref[...] * out_scale)
    y = jnp.concatenate(parts, axis=0).T.astype(BF16)
    dx = jnp.dot(y, wo_ref[...], preferred_element_type=F32)
    out_ref[...] = x_ref[...] + modv_ref[0:1, 2 * d:3 * d] * dx


def _diff_readout(xs, n_ctx, ot, modv, subln, wo, lambda_init):
    n, n_lat = ot.shape
    d = xs.shape[1]
    t = ROW_TILE
    ctx_tiles = n_ctx // t
    kern = functools.partial(_diff_readout_kernel, out_scale=1.0 - lambda_init)
    return pl.pallas_call(
        kern,
        grid=(n_lat // t,),
        in_specs=[pl.BlockSpec((t, d), lambda i: (i + ctx_tiles, 0)),
                  pl.BlockSpec((n, t), lambda i: (0, i)),
                  _const_spec(modv.shape), _const_spec((subln.shape[0], 1)), _const_spec(wo.shape)],
        out_specs=pl.BlockSpec((t, d), lambda i: (i, 0)),
        out_shape=jax.ShapeDtypeStruct((n_lat, d), F32),
        compiler_params=_cparams("parallel"),
        name="diff_readout",
    )(xs, ot, modv, subln.reshape(-1, 1), wo.astype(BF16))


def kernel(x, c, ctx, c_ctx, mod_w, mod_b, norm_mix, norm_ffn, gla_wq, gla_wk, gla_wv, gla_wr, gla_wg1, gla_wg2, gla_bg, gla_norm, gla_wo, diff_wq, diff_wk, diff_wv, diff_lq1, diff_lk1, diff_lq2, diff_lk2, diff_subln, diff_wo, ffn_wup, ffn_conv, ffn_conv_b, ffn_wdown, final_norm):
    bsz, n_lat, d = x.shape
    n_ctx = ctx.shape[1]
    depth = mod_w.shape[0]
    assert bsz == 1 and depth == 2, "layer 0 is the GLA mixer, layer 1 the differential attention"
    x2 = x.reshape(n_lat, d)
    ctx2 = ctx.reshape(n_ctx, d)
    cc = jnp.zeros((8, d), F32).at[0].set(c[0]).at[1].set(c_ctx)
    mod = _modulation(cc, mod_w, mod_b)

    q, k, v, gr, lg = _gla_proj(ctx2, x2, mod[0], norm_mix[0], gla_wq[0], gla_wk[0], gla_wv[0],
                                gla_wr[0], gla_wg1[0], gla_wg2[0], gla_bg[0])
    o_f, o_b = _gla_scan(q, k, v, lg, n_ctx)
    xs = _gla_readout(ctx2, x2, o_f, o_b, gr, mod[0], gla_norm[0], gla_wo[0])
    xs = _conv_ffn(xs, mod[0], norm_ffn[0], ffn_wup[0], ffn_conv[0], ffn_conv_b[0], ffn_wdown[0],
                   final_norm, ctx_tiles=n_ctx // ROW_TILE, final=False)

    lambda_init = 0.8 - 0.6 * math.exp(-0.3 * 1)
    qt, kk, vt = _diff_proj(xs, n_ctx, mod[1], norm_mix[1], diff_wq[0], diff_wk[0], diff_wv[0])
    lam_vecs = jnp.zeros((8, DIFF_HEAD_DIM), F32).at[0:4].set(
        jnp.concatenate([diff_lq1, diff_lk1, diff_lq2, diff_lk2], axis=0))
    ot = _diff_attn(qt, kk, vt, lam_vecs, lambda_init)
    xl = _diff_readout(xs, n_ctx, ot, mod[1], diff_subln[0], diff_wo[0], lambda_init)
    out = _conv_ffn(xl, mod[1], norm_ffn[1], ffn_wup[1], ffn_conv[1], ffn_conv_b[1], ffn_wdown[1],
                    final_norm, ctx_tiles=0, final=True)
    return out.reshape(bsz, n_lat, d)
```

```python
import functools
import math

import numpy as np
import jax
import jax.numpy as jnp
from jax import lax
from jax.experimental import pallas as pl
from jax.experimental.pallas import tpu as pltpu

GRID_W = 64
GLA_HEADS = 4
GLA_GATE_RANK = 16
GLA_GATE_NORM = 16.0
GLA_CHUNK = 64
DIFF_HEAD_DIM = 64
ROPE_BASE = 10000.0
CONV_WIDTH = 3
N_MOD = 6
EPS = 1e-6

LANES = 128
BF16_SUBLANES = 16
VMEM_LIMIT = 56 * 1024 * 1024

ROW_TILE = 256
FFN_COL_CHUNK = 256
ATT_TQ = 512
ATT_SUB = 512
ATT_TK = 1280
ATT_KW = 256
ATT_OVERFLOW_LOG2 = 64.0
MOD_COL_TILE = 1536

F32 = jnp.float32
BF16 = jnp.bfloat16

_NT = (((1,), (1,)), ((), ()))
_TN = (((0,), (0,)), ((), ()))


def _cparams(*sem):
    return pltpu.CompilerParams(dimension_semantics=sem, vmem_limit_bytes=VMEM_LIMIT)


def _const_spec(shape):
    nd = len(shape)
    return pl.BlockSpec(shape, lambda *_: (0,) * nd, pipeline_mode=pl.Buffered(1))


def _silu(x):
    return x * (1.0 / (1.0 + jnp.exp(-x)))


def _rms(x):
    return x * lax.rsqrt(jnp.mean(x * x, axis=-1, keepdims=True) + EPS)


def _mod_rows(modv_ref, is_ctx, k, d):
    sl = slice(k * d, (k + 1) * d)
    return jnp.where(is_ctx, modv_ref[1:2, sl], modv_ref[0:1, sl])


def _modnorm(x, g, scale, shift):
    return (_rms(x) * g) * (1.0 + scale) + shift


def _split2(x):
    hi = x.astype(BF16)
    lo = (x - hi.astype(F32)).astype(BF16)
    return hi, lo


def _mod_kernel(cc_ref, w_ref, b_ref, o_ref):
    s = _silu(cc_ref[...])
    w = w_ref[0]
    s_hi, s_lo = _split2(s)
    w_hi, w_lo = _split2(w)
    acc = jnp.dot(s_hi, w_hi, preferred_element_type=F32)
    acc += jnp.dot(s_lo, w_hi, preferred_element_type=F32)
    acc += jnp.dot(s_hi, w_lo, preferred_element_type=F32)
    o_ref[0] = acc + b_ref[0]


def _modulation(cc, mod_w, mod_b):
    depth, d, n = mod_w.shape
    tn = MOD_COL_TILE
    return pl.pallas_call(
        _mod_kernel,
        grid=(depth, n // tn),
        in_specs=[pl.BlockSpec((8, d), lambda l, j: (0, 0)),
                  pl.BlockSpec((1, d, tn), lambda l, j: (l, 0, j)),
                  pl.BlockSpec((1, 1, tn), lambda l, j: (l, 0, j))],
        out_specs=pl.BlockSpec((1, 8, tn), lambda l, j: (l, 0, j)),
        out_shape=jax.ShapeDtypeStruct((depth, 8, n), F32),
        compiler_params=_cparams("parallel", "parallel"),
        name="modulation",
    )(cc, mod_w, mod_b.reshape(depth, 1, n))


def _gla_proj_kernel(ctx_ref, x_ref, modv_ref, g_ref, w_ref, wg1_ref, wg2_ref, bg_ref,
                     q_ref, k_ref, v_ref, r_ref, lg_ref, *, dk_total, dv_total, q_scale):
    d = x_ref.shape[1]
    is_ctx = pl.program_id(0) == 0
    xt = jnp.where(is_ctx, ctx_ref[...], x_ref[...])
    h = _modnorm(xt, g_ref[...], _mod_rows(modv_ref, is_ctx, 1, d),
                 _mod_rows(modv_ref, is_ctx, 0, d)).astype(BF16)
    qkvr = jnp.dot(h, w_ref[...], preferred_element_type=F32)
    o1 = dk_total
    o2 = 2 * dk_total
    o3 = o2 + dv_total
    q_ref[...] = qkvr[:, :o1] * q_scale
    k_ref[...] = qkvr[:, o1:o2]
    v_ref[...] = qkvr[:, o2:o3].astype(BF16)
    r_ref[...] = _silu(qkvr[:, o3:]).astype(BF16)
    low = jnp.dot(h, wg1_ref[...], preferred_element_type=F32).astype(BF16)
    z = jnp.dot(low, wg2_ref[...], preferred_element_type=F32) + bg_ref[...]
    log_sig = jnp.minimum(z, 0.0) - jnp.log1p(jnp.exp(-jnp.abs(z)))
    lg_ref[...] = log_sig * (1.0 / GLA_GATE_NORM)


def _gla_proj(ctx2, x2, modv, norm_g, wq, wk, wv, wr, wg1, wg2, bg):
    n_ctx, d = ctx2.shape
    n_lat = x2.shape[0]
    t = ROW_TILE
    assert n_ctx == t and n_lat % t == 0
    n_all = n_ctx + n_lat
    dk_total = wq.shape[1]
    dv_total = wv.shape[1]
    rank = wg1.shape[2]
    w = jnp.concatenate([wq, wk, wv, wr], axis=1).astype(BF16)
    wg1c = jnp.zeros((d, LANES), F32).at[:, :rank].set(wg1[0]).at[:, rank:2 * rank].set(wg1[1])
    wg2c = (jnp.zeros((LANES, 2 * dk_total), F32)
            .at[:rank, :dk_total].set(wg2[0]).at[rank:2 * rank, dk_total:].set(wg2[1]))
    bgc = bg.reshape(1, 2 * dk_total)
    row = lambda i: (i, 0)
    kern = functools.partial(_gla_proj_kernel, dk_total=dk_total, dv_total=dv_total,
                             q_scale=(dk_total // GLA_HEADS) ** -0.5)
    return pl.pallas_call(
        kern,
        grid=(n_all // t,),
        in_specs=[pl.BlockSpec((t, d), lambda i: (0, 0)),
                  pl.BlockSpec((t, d), lambda i: (jnp.maximum(i - 1, 0), 0)),
                  _const_spec(modv.shape), _const_spec((1, d)),
                  _const_spec(w.shape), _const_spec(wg1c.shape), _const_spec(wg2c.shape),
                  _const_spec(bgc.shape)],
        out_specs=[pl.BlockSpec((t, dk_total), row), pl.BlockSpec((t, dk_total), row),
                   pl.BlockSpec((t, dv_total), row), pl.BlockSpec((t, dv_total), row),
                   pl.BlockSpec((t, 2 * dk_total), row)],
        out_shape=[jax.ShapeDtypeStruct((n_all, dk_total), F32),
                   jax.ShapeDtypeStruct((n_all, dk_total), F32),
                   jax.ShapeDtypeStruct((n_all, dv_total), BF16),
                   jax.ShapeDtypeStruct((n_all, dv_total), BF16),
                   jax.ShapeDtypeStruct((n_all, 2 * dk_total), F32)],
        compiler_params=_cparams("parallel"),
        name="gla_proj",
    )(ctx2, x2, modv, norm_g.reshape(1, d), w, wg1c.astype(BF16), wg2c.astype(BF16), bgc)


N_LEVELS = int(math.log2(GLA_CHUNK))
N_DECAY_BLOCKS = 2 + 2 * N_LEVELS


def _scan_tables():
    c = GLA_CHUNK
    t = np.arange(c)[:, None]
    u = np.arange(c)[None, :]
    blocks = [(u <= t), (u > t)]
    dq, dk, masks = [], [], [(t == u)]
    for lvl in range(1, N_LEVELS + 1):
        size = 1 << lvl
        half = size // 2
        start = (t // size) * size
        mid = start + half - 1
        upper = (t - start) >= half
        dq.append(upper & (u > mid) & (u <= t))
        dk.append((~upper) & (u > t) & (u <= mid))
        same = (t // size) == (u // size)
        masks.append(same & upper & ((u - (u // size) * size) < half))
    fwd = np.concatenate(blocks + dq + dk, axis=0).astype(np.float32)
    fmask = np.concatenate(masks, axis=0).astype(np.float32)
    rev = lambda m: m.reshape(-1, c, c)[:, ::-1, ::-1].reshape(-1, c)
    return (np.stack([fwd, rev(fwd)]), np.stack([fmask, rev(fmask)]))


def _scan_direction(q_ref, k_ref, v_ref, g_ref, dmat, mask_ref, direction, st_ref, o_ref):
    c = GLA_CHUNK
    dk = q_ref.shape[1] // GLA_HEADS
    dv = v_ref.shape[1] // GLA_HEADS
    g_hi, g_lo = _split2(g_ref[...])
    e = (jnp.dot(dmat, g_hi, preferred_element_type=F32)
         + jnp.dot(dmat, g_lo, preferred_element_type=F32))
    x = jnp.exp(e)
    last = c - 1 if direction == 0 else 0
    q_all = q_ref[...]
    k_all = k_ref[...]
    v_all = v_ref[...]
    blk = lambda n, hs: x[n * c:(n + 1) * c, hs]
    for h in range(GLA_HEADS):
        hs = slice(h * dk, (h + 1) * dk)
        q = q_all[:, hs]
        k = k_all[:, hs]
        v = v_all[:, h * dv:(h + 1) * dv]
        a = mask_ref[direction, 0:c, :] * lax.dot_general(
            q.astype(BF16), k.astype(BF16), _NT, preferred_element_type=F32)
        for lvl in range(1, N_LEVELS + 1):
            ql = (q * blk(1 + lvl, hs)).astype(BF16)
            kl = (k * blk(1 + N_LEVELS + lvl, hs)).astype(BF16)
            a += mask_ref[direction, lvl * c:(lvl + 1) * c, :] * lax.dot_general(
                ql, kl, _NT, preferred_element_type=F32)
        st = st_ref[direction, h]
        o = jnp.dot(a.astype(BF16), v, preferred_element_type=F32)
        o += lax.dot_general((q * blk(0, hs)).astype(BF16), st.astype(BF16), _NT,
                             preferred_element_type=F32)
        o_ref[:, h * dv:(h + 1) * dv] = o
        k_end = (k * blk(1, hs)).astype(BF16)
        decay = x[last:last + 1, hs]
        st_ref[direction, h] = st * decay + lax.dot_general(
            v, k_end, _TN, preferred_element_type=F32)


def _gla_scan_kernel(dmat_ref, mask_ref, qf, kf, vf, gf, qb, kb, vb, gb, of_ref, ob_ref, st_ref):
    @pl.when(pl.program_id(0) == 0)
    def _():
        st_ref[...] = jnp.zeros_like(st_ref)

    _scan_direction(qf, kf, vf, gf, dmat_ref[0], mask_ref, 0, st_ref, of_ref)
    _scan_direction(qb, kb, vb, gb, dmat_ref[1], mask_ref, 1, st_ref, ob_ref)


def _gla_scan(q, k, v, lg, n_ctx):
    n_all, dk_total = q.shape
    dv_total = v.shape[1]
    c = GLA_CHUNK
    n_chunks = n_all // c
    ctx_chunks = n_ctx // c
    dmat, masks = _scan_tables()
    fwd = lambda i: (i, 0)
    bidx = lambda i: jnp.where(i < ctx_chunks, ctx_chunks - 1 - i, n_chunks + ctx_chunks - 1 - i)
    bwd = lambda i: (bidx(i), 0)
    bwd_g = lambda i: (bidx(i), 1)
    return pl.pallas_call(
        _gla_scan_kernel,
        grid=(n_chunks,),
        in_specs=[_const_spec(dmat.shape), _const_spec(masks.shape),
                  pl.BlockSpec((c, dk_total), fwd), pl.BlockSpec((c, dk_total), fwd),
                  pl.BlockSpec((c, dv_total), fwd), pl.BlockSpec((c, dk_total), fwd),
                  pl.BlockSpec((c, dk_total), bwd), pl.BlockSpec((c, dk_total), bwd),
                  pl.BlockSpec((c, dv_total), bwd), pl.BlockSpec((c, dk_total), bwd_g)],
        out_specs=[pl.BlockSpec((c, dv_total), fwd), pl.BlockSpec((c, dv_total), bwd)],
        out_shape=[jax.ShapeDtypeStruct((n_all, dv_total), F32),
                   jax.ShapeDtypeStruct((n_all, dv_total), F32)],
        scratch_shapes=[pltpu.VMEM((2, GLA_HEADS, dv_total // GLA_HEADS, dk_total // GLA_HEADS), F32)],
        compiler_params=_cparams("arbitrary"),
        name="gla_scan",
    )(jnp.asarray(dmat, BF16), jnp.asarray(masks, F32), q, k, v, lg, q, k, v, lg)


def _gla_readout_kernel(ctx_ref, x_ref, of_ref, ob_ref, r_ref, modv_ref, gn_ref, wo_ref, out_ref):
    d = x_ref.shape[1]
    dv = gn_ref.shape[1]
    is_ctx = pl.program_id(0) == 0
    xt = jnp.where(is_ctx, ctx_ref[...], x_ref[...])
    o = of_ref[...] + ob_ref[...]
    parts = []
    for h in range(GLA_HEADS):
        parts.append(_rms(o[:, h * dv:(h + 1) * dv]) * gn_ref[...])
    y = (jnp.concatenate(parts, axis=1) * r_ref[...].astype(F32)).astype(BF16)
    dx = jnp.dot(y, wo_ref[...], preferred_element_type=F32)
    out_ref[...] = xt + _mod_rows(modv_ref, is_ctx, 2, d) * dx


def _gla_readout(ctx2, x2, o_f, o_b, gr, modv, gn, wo):
    n_ctx, d = ctx2.shape
    n_all, dv_total = o_f.shape
    t = ROW_TILE
    row = lambda i: (i, 0)
    return pl.pallas_call(
        _gla_readout_kernel,
        grid=(n_all // t,),
        in_specs=[pl.BlockSpec((t, d), lambda i: (0, 0)),
                  pl.BlockSpec((t, d), lambda i: (jnp.maximum(i - 1, 0), 0)),
                  pl.BlockSpec((t, dv_total), row), pl.BlockSpec((t, dv_total), row),
                  pl.BlockSpec((t, dv_total), row),
                  _const_spec(modv.shape), _const_spec((1, gn.shape[0])), _const_spec(wo.shape)],
        out_specs=pl.BlockSpec((t, d), row),
        out_shape=jax.ShapeDtypeStruct((n_all, d), F32),
        compiler_params=_cparams("parallel"),
        name="gla_readout",
    )(ctx2, x2, o_f, o_b, gr, modv, gn.reshape(1, -1), wo.astype(BF16))


HALO = BF16_SUBLANES


def _ffn_kernel(prev_ref, x_ref, next_ref, modv_ref, g_ref, wup_ref, cw_ref, cb_ref, wdn_ref,
                fin_ref, out_ref, lhs_ref, ua_ref, ub_ref, acc_ref, *, ctx_tiles, final):
    t, d = x_ref.shape
    d_ff = wdn_ref.shape[0]
    i = pl.program_id(0)
    last = pl.num_programs(0) - 1
    is_ctx = i < ctx_tiles
    has_prev = jnp.logical_and(i != 0, i != ctx_tiles)
    has_next = jnp.logical_and(i != last, i != ctx_tiles - 1)
    g = g_ref[...]
    scale = _mod_rows(modv_ref, is_ctx, 4, d)
    shift = _mod_rows(modv_ref, is_ctx, 3, d)
    xt = x_ref[...]
    lhs_ref[0:HALO, :] = jnp.where(has_prev, _modnorm(prev_ref[...], g, scale, shift), 0.0).astype(BF16)
    lhs_ref[HALO:HALO + t, :] = _modnorm(xt, g, scale, shift).astype(BF16)
    lhs_ref[HALO + t:, :] = jnp.where(has_next, _modnorm(next_ref[...], g, scale, shift), 0.0).astype(BF16)
    lhs = lhs_ref[...]
    cc = FFN_COL_CHUNK
    for c in range(d_ff // cc):
        halves = []
        for u_ref, off in ((ua_ref, c * cc), (ub_ref, d_ff + c * cc)):
            cols = slice(off, off + cc)
            u_ref[...] = jnp.dot(lhs, wup_ref[:, cols], preferred_element_type=F32)
            halves.append(u_ref[HALO - 1:HALO - 1 + t, :] * cw_ref[0:1, cols]
                          + u_ref[HALO:HALO + t, :] * cw_ref[1:2, cols]
                          + u_ref[HALO + 1:HALO + 1 + t, :] * cw_ref[2:3, cols]
                          + cb_ref[:, cols])
        act = (_silu(halves[0]) * halves[1]).astype(BF16)
        part = jnp.dot(act, wdn_ref[c * cc:(c + 1) * cc, :], preferred_element_type=F32)
        if c == 0:
            acc_ref[...] = part
        else:
            acc_ref[...] += part
    y = xt + _mod_rows(modv_ref, is_ctx, 5, d) * acc_ref[...]
    if final:
        y = _rms(y) * fin_ref[...]
    out_ref[...] = y


def _conv_ffn(xs, modv, norm_g, w_up, w_conv, b_conv, w_down, final_g, *, ctx_tiles, final):
    n, d = xs.shape
    d_ff = w_down.shape[0]
    t = ROW_TILE
    assert d_ff % FFN_COL_CHUNK == 0
    per = t // HALO
    n_halo_blocks = n // HALO
    kern = functools.partial(_ffn_kernel, ctx_tiles=ctx_tiles, final=final)
    return pl.pallas_call(
        kern,
        grid=(n // t,),
        in_specs=[pl.BlockSpec((HALO, d), lambda i: (jnp.maximum(i * per - 1, 0), 0)),
                  pl.BlockSpec((t, d), lambda i: (i, 0)),
                  pl.BlockSpec((HALO, d), lambda i: (jnp.minimum((i + 1) * per, n_halo_blocks - 1), 0)),
                  _const_spec(modv.shape), _const_spec((1, d)),
                  _const_spec(w_up.shape), _const_spec(w_conv.shape), _const_spec((1, 2 * d_ff)),
                  _const_spec(w_down.shape), _const_spec((1, d))],
        out_specs=pl.BlockSpec((t, d), lambda i: (i, 0)),
        out_shape=jax.ShapeDtypeStruct((n, d), F32),
        scratch_shapes=[pltpu.VMEM((t + 2 * HALO, d), BF16),
                        pltpu.VMEM((t + 2 * HALO, FFN_COL_CHUNK), F32),
                        pltpu.VMEM((t + 2 * HALO, FFN_COL_CHUNK), F32),
                        pltpu.VMEM((t, d), F32)],
        compiler_params=_cparams("parallel"),
        name="conv_ffn_final" if final else "conv_ffn",
    )(xs, xs, xs, modv, norm_g.reshape(1, d), w_up.astype(BF16), w_conv,
      b_conv.reshape(1, -1), w_down.astype(BF16), final_g.reshape(1, d))


def _rope_tables(n_tokens, width):
    rows = n_tokens // GRID_W
    row = jnp.repeat(jnp.arange(rows, dtype=F32), GRID_W)
    col = jnp.tile(jnp.arange(GRID_W, dtype=F32), rows)
    quarter = DIFF_HEAD_DIM // 4
    inv = ROPE_BASE ** (-jnp.arange(quarter, dtype=F32) / quarter)
    ang_r = row[:, None] * inv
    ang_c = col[:, None] * inv
    ang = jnp.concatenate([ang_r, ang_r, ang_c, ang_c], axis=-1)
    sign = jnp.tile(jnp.concatenate([-jnp.ones((quarter,), F32), jnp.ones((quarter,), F32)]), 2)
    reps = width // DIFF_HEAD_DIM
    return jnp.tile(jnp.cos(ang), (1, reps)), jnp.tile(jnp.sin(ang) * sign, (1, reps))


def _diff_proj_kernel(x_ref, cos_ref, sin_ref, modv_ref, g_ref, w_ref, qt_ref, k_ref, vt_ref):
    t, d = x_ref.shape
    is_ctx = pl.program_id(0) == 0
    h = _modnorm(x_ref[...], g_ref[...], _mod_rows(modv_ref, is_ctx, 1, d),
                 _mod_rows(modv_ref, is_ctx, 0, d)).astype(BF16)
    qkv = jnp.dot(h, w_ref[...], preferred_element_type=F32)
    n = qkv.shape[1] // 3
    quarter = DIFF_HEAD_DIM // 4
    reps = n // LANES
    cos = jnp.concatenate([cos_ref[...]] * reps, axis=1)
    sin = jnp.concatenate([sin_ref[...]] * reps, axis=1)
    lane = lax.broadcasted_iota(jnp.int32, (t, n), 1)
    first_half = (lane % (2 * quarter)) < quarter

    def rope(a):
        partner = jnp.where(first_half, pltpu.roll(a, n - quarter, 1), pltpu.roll(a, quarter, 1))
        return jnp.where(is_ctx, a, a * cos + partner * sin)

    q = rope(qkv[:, :n]) * (DIFF_HEAD_DIM ** -0.5 * math.log2(math.e))
    qt_ref[...] = q.T.astype(BF16)
    k = rope(qkv[:, n:2 * n])
    hw = 2 * DIFF_HEAD_DIM
    ones_col = jnp.where(lax.broadcasted_iota(jnp.int32, (t, ATT_KW - hw), 1) == 0, 1.0, 0.0)
    parts = []
    for hh in range(n // hw):
        parts += [k[:, hh * hw:(hh + 1) * hw], ones_col]
    k_ref[...] = jnp.concatenate(parts, axis=1).astype(BF16)
    vt_ref[...] = qkv[:, 2 * n:].T.astype(BF16)


def _diff_proj(xs, n_ctx, modv, norm_g, wq, wk, wv):
    n_all, d = xs.shape
    n_lat = n_all - n_ctx
    t = ROW_TILE
    n = wq.shape[1]
    w = jnp.concatenate([wq, wk, wv], axis=1).astype(BF16)
    cos, sin = _rope_tables(n_lat, LANES)
    kaug = n // (2 * DIFF_HEAD_DIM) * ATT_KW
    lat = lambda i: (jnp.maximum(i - 1, 0), 0)
    return pl.pallas_call(
        _diff_proj_kernel,
        grid=(n_all // t,),
        in_specs=[pl.BlockSpec((t, d), lambda i: (i, 0)),
                  pl.BlockSpec((t, LANES), lat), pl.BlockSpec((t, LANES), lat),
                  _const_spec(modv.shape), _const_spec((1, d)), _const_spec(w.shape)],
        out_specs=[pl.BlockSpec((n, t), lambda i: (0, jnp.maximum(i - 1, 0))),
                   pl.BlockSpec((t, kaug), lambda i: (i, 0)),
                   pl.BlockSpec((n, t), lambda i: (0, i))],
        out_shape=[jax.ShapeDtypeStruct((n, n_lat), BF16),
                   jax.ShapeDtypeStruct((n_all, kaug), BF16),
                   jax.ShapeDtypeStruct((n, n_all), BF16)],
        compiler_params=_cparams("arbitrary"),
        name="diff_proj",
    )(xs, cos, sin, modv, norm_g.reshape(1, d), w)


SHIFT_ROWS = BF16_SUBLANES


def _diff_attn_kernel(lam_ref, qt_ref, k_ref, vt_ref, o_ref, qa_ref, sh_ref, l_ref, acc_ref,
                      *, lambda_init):
    hd = DIFF_HEAD_DIM
    hw = 2 * hd
    tq = qt_ref.shape[1]
    kw = k_ref.shape[1]
    n_keys = k_ref.shape[0]
    tk = ATT_TK
    qt = qt_ref[...]
    rows = lax.broadcasted_iota(jnp.int32, qt.shape, 0)
    zero = jnp.zeros_like(qt)
    shift_row = lax.broadcasted_iota(jnp.int32, (SHIFT_ROWS, tq), 0) == 0

    def set_shift(br, value):
        sh_ref[br] = value
        qa_ref[br, hw:hw + SHIFT_ROWS, :] = jnp.where(shift_row, -value, 0.0).astype(BF16)

    def round_bf16(v):
        return v.astype(BF16).astype(F32)

    for br in range(2):
        keep = jnp.logical_and(rows >= br * hd, rows < (br + 1) * hd)
        qa_ref[br, 0:hw, :] = jnp.where(keep, qt, zero)
        qa_ref[br, hw:, :] = jnp.zeros((kw - hw, tq), BF16)
    l_ref[...] = jnp.zeros_like(l_ref)
    acc_ref[...] = jnp.zeros_like(acc_ref)
    k0 = k_ref[0:ATT_SUB, :]
    for br in range(2):
        s0 = jnp.dot(k0, qa_ref[br], preferred_element_type=F32)
        set_shift(br, round_bf16(jnp.max(s0, axis=0, keepdims=True)))

    def block(j):
        start = pl.multiple_of(j * tk, LANES)
        kb = k_ref[pl.ds(start, tk), :]
        vb = vt_ref[:, pl.ds(start, tk)]
        out = []
        for br in range(2):
            bm, cs, pv = [], [], []
            for sub in range(tq // ATT_SUB):
                cols = slice(sub * ATT_SUB, (sub + 1) * ATT_SUB)
                s = jnp.dot(kb, qa_ref[br, :, cols], preferred_element_type=F32)
                bm.append(jnp.max(s, axis=0, keepdims=True))
                p = jnp.exp2(s)
                cs.append(jnp.sum(p, axis=0, keepdims=True))
                pv.append(jnp.dot(vb, p.astype(BF16), preferred_element_type=F32))
            out += [jnp.concatenate(bm, axis=1), jnp.concatenate(cs, axis=1),
                    jnp.concatenate(pv, axis=1)]
        return tuple(out)

    def advance_shift(br, bm, extra_l, extra_acc):
        old = sh_ref[br]
        new = round_bf16(old + jnp.maximum(bm, 0.0))
        alpha = jnp.exp2(old - new)
        l_ref[br] = (l_ref[br] + extra_l) * alpha
        acc_ref[br] = (acc_ref[br] + extra_acc) * alpha
        set_shift(br, new)

    def body(j, carry):
        res = block(j)
        overflow = jnp.max(jnp.maximum(res[0], res[3])) > ATT_OVERFLOW_LOG2

        def redo():
            for br in range(2):
                advance_shift(br, res[3 * br], 0.0, 0.0)
            return block(j)

        res = lax.cond(overflow, redo, lambda: res)
        for br in range(2):
            advance_shift(br, res[3 * br], res[3 * br + 1], res[3 * br + 2])
        return carry

    lax.fori_loop(0, n_keys // tk, body, 0)
    lam = (jnp.exp(jnp.sum(lam_ref[0:1, :] * lam_ref[1:2, :], axis=1, keepdims=True))
           - jnp.exp(jnp.sum(lam_ref[2:3, :] * lam_ref[3:4, :], axis=1, keepdims=True))
           + lambda_init)
    o_ref[...] = acc_ref[0] / l_ref[0] - lam * (acc_ref[1] / l_ref[1])


def _diff_attn(qt, k, vt, lam_vecs, lambda_init):
    n, n_lat = qt.shape
    n_all = k.shape[0]
    hw = 2 * DIFF_HEAD_DIM
    heads = n // hw
    tq = ATT_TQ
    assert n_lat % tq == 0 and n_all % ATT_TK == 0 and tq % ATT_SUB == 0
    kern = functools.partial(_diff_attn_kernel, lambda_init=lambda_init)
    return pl.pallas_call(
        kern,
        grid=(heads, n_lat // tq),
        in_specs=[pl.BlockSpec(lam_vecs.shape, lambda h, i: (0, 0)),
                  pl.BlockSpec((hw, tq), lambda h, i: (h, i)),
                  pl.BlockSpec((n_all, ATT_KW), lambda h, i: (0, h)),
                  pl.BlockSpec((hw, n_all), lambda h, i: (h, 0))],
        out_specs=pl.BlockSpec((hw, tq), lambda h, i: (h, i)),
        out_shape=jax.ShapeDtypeStruct((n, n_lat), F32),
        scratch_shapes=[pltpu.VMEM((2, ATT_KW, tq), BF16), pltpu.VMEM((2, 1, tq), F32),
                        pltpu.VMEM((2, 1, tq), F32), pltpu.VMEM((2, hw, tq), F32)],
        compiler_params=_cparams("arbitrary", "arbitrary"),
        name="diff_attn",
    )(lam_vecs, qt, k, vt)


def _diff_readout_kernel(x_ref, ot_ref, modv_ref, sub_ref, wo_ref, out_ref, *, out_scale):
    d = x_ref.shape[1]
    hw = sub_ref.shape[0]
    ot = ot_ref[...]
    parts = []
    for h in range(ot.shape[0] // hw):
        oh = ot[h * hw:(h + 1) * hw, :]
        ms = jnp.mean(oh * oh, axis=0, keepdims=True)
        parts.append(oh * lax.rsqrt(ms + EPS) * sub_ref[...] * out_scale)
    y = jnp.concatenate(parts, axis=0).T.astype(BF16)
    dx = jnp.dot(y, wo_ref[...], preferred_element_type=F32)
    out_ref[...] = x_ref[...] + modv_ref[0:1, 2 * d:3 * d] * dx


def _diff_readout(xs, n_ctx, ot, modv, subln, wo, lambda_init):
    n, n_lat = ot.shape
    d = xs.shape[1]
    t = ROW_TILE
    ctx_tiles = n_ctx // t
    kern = functools.partial(_diff_readout_kernel, out_scale=1.0 - lambda_init)
    return pl.pallas_call(
        kern,
        grid=(n_lat // t,),
        in_specs=[pl.BlockSpec((t, d), lambda i: (i + ctx_tiles, 0)),
                  pl.BlockSpec((n, t), lambda i: (0, i)),
                  _const_spec(modv.shape), _const_spec((subln.shape[0], 1)), _const_spec(wo.shape)],
        out_specs=pl.BlockSpec((t, d), lambda i: (i, 0)),
        out_shape=jax.ShapeDtypeStruct((n_lat, d), F32),
        compiler_params=_cparams("parallel"),
        name="diff_readout",
    )(xs, ot, modv, subln.reshape(-1, 1), wo.astype(BF16))


def kernel(x, c, ctx, c_ctx, mod_w, mod_b, norm_mix, norm_ffn, gla_wq, gla_wk, gla_wv, gla_wr, gla_wg1, gla_wg2, gla_bg, gla_norm, gla_wo, diff_wq, diff_wk, diff_wv, diff_lq1, diff_lk1, diff_lq2, diff_lk2, diff_subln, diff_wo, ffn_wup, ffn_conv, ffn_conv_b, ffn_wdown, final_norm):
    bsz, n_lat, d = x.shape
    n_ctx = ctx.shape[1]
    depth = mod_w.shape[0]
    assert bsz == 1 and depth == 2, "layer 0 is the GLA mixer, layer 1 the differential attention"
    x2 = x.reshape(n_lat, d)
    ctx2 = ctx.reshape(n_ctx, d)
    cc = jnp.zeros((8, d), F32).at[0].set(c[0]).at[1].set(c_ctx)
    mod = _modulation(cc, mod_w, mod_b)

    q, k, v, gr, lg = _gla_proj(ctx2, x2, mod[0], norm_mix[0], gla_wq[0], gla_wk[0], gla_wv[0],
                                gla_wr[0], gla_wg1[0], gla_wg2[0], gla_bg[0])
    o_f, o_b = _gla_scan(q, k, v, lg, n_ctx)
    xs = _gla_readout(ctx2, x2, o_f, o_b, gr, mod[0], gla_norm[0], gla_wo[0])
    xs = _conv_ffn(xs, mod[0], norm_ffn[0], ffn_wup[0], ffn_conv[0], ffn_conv_b[0], ffn_wdown[0],
                   final_norm, ctx_tiles=n_ctx // ROW_TILE, final=False)

    lambda_init = 0.8 - 0.6 * math.exp(-0.3 * 1)
    qt, kk, vt = _diff_proj(xs, n_ctx, mod[1], norm_mix[1], diff_wq[0], diff_wk[0], diff_wv[0])
    lam_vecs = jnp.zeros((8, DIFF_HEAD_DIM), F32).at[0:4].set(
        jnp.concatenate([diff_lq1, diff_lk1, diff_lq2, diff_lk2], axis=0))
    ot = _diff_attn(qt, kk, vt, lam_vecs, lambda_init)
    xl = _diff_readout(xs, n_ctx, ot, mod[1], diff_subln[0], diff_wo[0], lambda_init)
    out = _conv_ffn(xl, mod[1], norm_ffn[1], ffn_wup[1], ffn_conv[1], ffn_conv_b[1], ffn_wdown[1],
                    final_norm, ctx_tiles=0, final=True)
    return out.reshape(bsz, n_lat, d)
```

```python
import functools
import math

import numpy as np
import jax
import jax.numpy as jnp
from jax import lax
from jax.experimental import pallas as pl
from jax.experimental.pallas import tpu as pltpu

GRID_W = 64
GLA_HEADS = 4
GLA_GATE_RANK = 16
GLA_GATE_NORM = 16.0
GLA_CHUNK = 64
DIFF_HEAD_DIM = 64
ROPE_BASE = 10000.0
CONV_WIDTH = 3
N_MOD = 6
EPS = 1e-6

LANES = 128
BF16_SUBLANES = 16
VMEM_LIMIT = 56 * 1024 * 1024

ROW_TILE = 256
FFN_COL_CHUNK = 256
ATT_TQ = 512
ATT_SUB = 512
ATT_TK = 3328
ATT_KW = 256
ATT_OVERFLOW_LOG2 = 64.0
MOD_COL_TILE = 1536

F32 = jnp.float32
BF16 = jnp.bfloat16

_NT = (((1,), (1,)), ((), ()))
_TN = (((0,), (0,)), ((), ()))


def _cparams(*sem):
    return pltpu.CompilerParams(dimension_semantics=sem, vmem_limit_bytes=VMEM_LIMIT)


def _const_spec(shape):
    nd = len(shape)
    return pl.BlockSpec(shape, lambda *_: (0,) * nd, pipeline_mode=pl.Buffered(1))


def _silu(x):
    return x * (1.0 / (1.0 + jnp.exp(-x)))


def _rms(x):
    return x * lax.rsqrt(jnp.mean(x * x, axis=-1, keepdims=True) + EPS)


def _mod_rows(modv_ref, is_ctx, k, d):
    sl = slice(k * d, (k + 1) * d)
    return jnp.where(is_ctx, modv_ref[1:2, sl], modv_ref[0:1, sl])


def _modnorm(x, g, scale, shift):
    return (_rms(x) * g) * (1.0 + scale) + shift


def _split2(x):
    hi = x.astype(BF16)
    lo = (x - hi.astype(F32)).astype(BF16)
    return hi, lo


def _mod_kernel(cc_ref, w_ref, b_ref, o_ref):
    s = _silu(cc_ref[...])
    w = w_ref[0]
    s_hi, s_lo = _split2(s)
    w_hi, w_lo = _split2(w)
    acc = jnp.dot(s_hi, w_hi, preferred_element_type=F32)
    acc += jnp.dot(s_lo, w_hi, preferred_element_type=F32)
    acc += jnp.dot(s_hi, w_lo, preferred_element_type=F32)
    o_ref[0] = acc + b_ref[0]


def _modulation(cc, mod_w, mod_b):
    depth, d, n = mod_w.shape
    tn = MOD_COL_TILE
    return pl.pallas_call(
        _mod_kernel,
        grid=(depth, n // tn),
        in_specs=[pl.BlockSpec((8, d), lambda l, j: (0, 0)),
                  pl.BlockSpec((1, d, tn), lambda l, j: (l, 0, j)),
                  pl.BlockSpec((1, 1, tn), lambda l, j: (l, 0, j))],
        out_specs=pl.BlockSpec((1, 8, tn), lambda l, j: (l, 0, j)),
        out_shape=jax.ShapeDtypeStruct((depth, 8, n), F32),
        compiler_params=_cparams("parallel", "parallel"),
        name="modulation",
    )(cc, mod_w, mod_b.reshape(depth, 1, n))


def _gla_proj_kernel(ctx_ref, x_ref, modv_ref, g_ref, w_ref, wg1_ref, wg2_ref, bg_ref,
                     q_ref, k_ref, v_ref, r_ref, lg_ref, *, dk_total, dv_total, q_scale):
    d = x_ref.shape[1]
    is_ctx = pl.program_id(0) == 0
    xt = jnp.where(is_ctx, ctx_ref[...], x_ref[...])
    h = _modnorm(xt, g_ref[...], _mod_rows(modv_ref, is_ctx, 1, d),
                 _mod_rows(modv_ref, is_ctx, 0, d)).astype(BF16)
    qkvr = jnp.dot(h, w_ref[...], preferred_element_type=F32)
    o1 = dk_total
    o2 = 2 * dk_total
    o3 = o2 + dv_total
    q_ref[...] = qkvr[:, :o1] * q_scale
    k_ref[...] = qkvr[:, o1:o2]
    v_ref[...] = qkvr[:, o2:o3].astype(BF16)
    r_ref[...] = _silu(qkvr[:, o3:]).astype(BF16)
    low = jnp.dot(h, wg1_ref[...], preferred_element_type=F32).astype(BF16)
    z = jnp.dot(low, wg2_ref[...], preferred_element_type=F32) + bg_ref[...]
    log_sig = jnp.minimum(z, 0.0) - jnp.log1p(jnp.exp(-jnp.abs(z)))
    lg_ref[...] = log_sig * (1.0 / GLA_GATE_NORM)


def _gla_proj(ctx2, x2, modv, norm_g, wq, wk, wv, wr, wg1, wg2, bg):
    n_ctx, d = ctx2.shape
    n_lat = x2.shape[0]
    t = ROW_TILE
    assert n_ctx == t and n_lat % t == 0
    n_all = n_ctx + n_lat
    dk_total = wq.shape[1]
    dv_total = wv.shape[1]
    rank = wg1.shape[2]
    w = jnp.concatenate([wq, wk, wv, wr], axis=1).astype(BF16)
    wg1c = jnp.zeros((d, LANES), F32).at[:, :rank].set(wg1[0]).at[:, rank:2 * rank].set(wg1[1])
    wg2c = (jnp.zeros((LANES, 2 * dk_total), F32)
            .at[:rank, :dk_total].set(wg2[0]).at[rank:2 * rank, dk_total:].set(wg2[1]))
    bgc = bg.reshape(1, 2 * dk_total)
    row = lambda i: (i, 0)
    kern = functools.partial(_gla_proj_kernel, dk_total=dk_total, dv_total=dv_total,
                             q_scale=(dk_total // GLA_HEADS) ** -0.5)
    return pl.pallas_call(
        kern,
        grid=(n_all // t,),
        in_specs=[pl.BlockSpec((t, d), lambda i: (0, 0)),
                  pl.BlockSpec((t, d), lambda i: (jnp.maximum(i - 1, 0), 0)),
                  _const_spec(modv.shape), _const_spec((1, d)),
                  _const_spec(w.shape), _const_spec(wg1c.shape), _const_spec(wg2c.shape),
                  _const_spec(bgc.shape)],
        out_specs=[pl.BlockSpec((t, dk_total), row), pl.BlockSpec((t, dk_total), row),
                   pl.BlockSpec((t, dv_total), row), pl.BlockSpec((t, dv_total), row),
                   pl.BlockSpec((t, 2 * dk_total), row)],
        out_shape=[jax.ShapeDtypeStruct((n_all, dk_total), F32),
                   jax.ShapeDtypeStruct((n_all, dk_total), F32),
                   jax.ShapeDtypeStruct((n_all, dv_total), BF16),
                   jax.ShapeDtypeStruct((n_all, dv_total), BF16),
                   jax.ShapeDtypeStruct((n_all, 2 * dk_total), F32)],
        compiler_params=_cparams("parallel"),
        name="gla_proj",
    )(ctx2, x2, modv, norm_g.reshape(1, d), w, wg1c.astype(BF16), wg2c.astype(BF16), bgc)


N_LEVELS = int(math.log2(GLA_CHUNK))
N_DECAY_BLOCKS = 2 + N_LEVELS


def _scan_tables():
    c = GLA_CHUNK
    t = np.arange(c)[:, None]
    u = np.arange(c)[None, :]
    blocks = [(u <= t), (u > t)]
    masks = [(t == u)]
    for lvl in range(1, N_LEVELS + 1):
        size = 1 << lvl
        half = size // 2
        start = (t // size) * size
        mid = start + half - 1
        upper = (t - start) >= half
        blocks.append((upper & (u > mid) & (u <= t)) | ((~upper) & (u > t) & (u <= mid)))
        same = (t // size) == (u // size)
        masks.append(same & upper & ((u - (u // size) * size) < half))
    fwd = np.concatenate(blocks, axis=0).astype(np.float32)
    fmask = np.concatenate(masks, axis=0).astype(np.float32)
    rev = lambda m: m.reshape(-1, c, c)[:, ::-1, ::-1].reshape(-1, c)
    twice = lambda m: np.concatenate([m, m], axis=1)
    return (np.stack([twice(fwd), twice(rev(fwd))]), np.stack([fmask, rev(fmask)]))


def _scan_direction(q_ref, k_ref, v_ref, g_ref, dmat, mask_ref, direction, st_ref, o_ref):
    c = GLA_CHUNK
    n_sub = q_ref.shape[0] // c
    dk = q_ref.shape[1] // GLA_HEADS
    dv = v_ref.shape[1] // GLA_HEADS
    last = c - 1 if direction == 0 else 0
    order = range(n_sub) if direction == 0 else range(n_sub - 1, -1, -1)
    for ci in order:
        rows = slice(ci * c, (ci + 1) * c)
        g_hi, g_lo = _split2(g_ref[rows, :])
        e = jnp.dot(dmat, jnp.concatenate([g_hi, g_lo], axis=0),
                    preferred_element_type=F32)
        x = jnp.exp(e)
        q_all = q_ref[rows, :]
        k_all = k_ref[rows, :]
        v_all = v_ref[rows, :]
        blk = lambda n, hs: x[n * c:(n + 1) * c, hs]
        for h in range(GLA_HEADS):
            hs = slice(h * dk, (h + 1) * dk)
            q = q_all[:, hs]
            k = k_all[:, hs]
            v = v_all[:, h * dv:(h + 1) * dv]
            a = mask_ref[direction, 0:c, :] * lax.dot_general(
                q.astype(BF16), k.astype(BF16), _NT, preferred_element_type=F32)
            for lvl in range(1, N_LEVELS + 1):
                f = blk(1 + lvl, hs)
                a += mask_ref[direction, lvl * c:(lvl + 1) * c, :] * lax.dot_general(
                    (q * f).astype(BF16), (k * f).astype(BF16), _NT, preferred_element_type=F32)
            st = st_ref[direction, h]
            o = jnp.dot(a.astype(BF16), v, preferred_element_type=F32)
            o += lax.dot_general((q * blk(0, hs)).astype(BF16), st.astype(BF16), _NT,
                                 preferred_element_type=F32)
            o_ref[rows, h * dv:(h + 1) * dv] = o
            k_end = (k * blk(1, hs)).astype(BF16)
            decay = x[last:last + 1, hs]
            st_ref[direction, h] = st * decay + lax.dot_general(
                v, k_end, _TN, preferred_element_type=F32)


def _gla_scan_kernel(dmat_ref, mask_ref, qf, kf, vf, gf, qb, kb, vb, gb, of_ref, ob_ref, st_ref):
    @pl.when(pl.program_id(0) == 0)
    def _():
        st_ref[...] = jnp.zeros_like(st_ref)

    _scan_direction(qf, kf, vf, gf, dmat_ref[0], mask_ref, 0, st_ref, of_ref)
    _scan_direction(qb, kb, vb, gb, dmat_ref[1], mask_ref, 1, st_ref, ob_ref)


def _gla_scan(q, k, v, lg, n_ctx):
    n_all, dk_total = q.shape
    dv_total = v.shape[1]
    t = ROW_TILE
    assert t % GLA_CHUNK == 0 and n_ctx == t
    n_tiles = n_all // t
    dmat, masks = _scan_tables()
    fwd = lambda i: (i, 0)
    bidx = lambda i: jnp.where(i == 0, 0, n_tiles - i)
    bwd = lambda i: (bidx(i), 0)
    bwd_g = lambda i: (bidx(i), 1)
    return pl.pallas_call(
        _gla_scan_kernel,
        grid=(n_tiles,),
        in_specs=[_const_spec(dmat.shape), _const_spec(masks.shape),
                  pl.BlockSpec((t, dk_total), fwd), pl.BlockSpec((t, dk_total), fwd),
                  pl.BlockSpec((t, dv_total), fwd), pl.BlockSpec((t, dk_total), fwd),
                  pl.BlockSpec((t, dk_total), bwd), pl.BlockSpec((t, dk_total), bwd),
                  pl.BlockSpec((t, dv_total), bwd), pl.BlockSpec((t, dk_total), bwd_g)],
        out_specs=[pl.BlockSpec((t, dv_total), fwd), pl.BlockSpec((t, dv_total), bwd)],
        out_shape=[jax.ShapeDtypeStruct((n_all, dv_total), F32),
                   jax.ShapeDtypeStruct((n_all, dv_total), F32)],
        scratch_shapes=[pltpu.VMEM((2, GLA_HEADS, dv_total // GLA_HEADS, dk_total // GLA_HEADS), F32)],
        compiler_params=_cparams("arbitrary"),
        name="gla_scan",
    )(jnp.asarray(dmat, BF16), jnp.asarray(masks, F32), q, k, v, lg, q, k, v, lg)


def _gla_readout_kernel(ctx_ref, x_ref, of_ref, ob_ref, r_ref, modv_ref, gn_ref, wo_ref, out_ref):
    d = x_ref.shape[1]
    dv = gn_ref.shape[1]
    is_ctx = pl.program_id(0) == 0
    xt = jnp.where(is_ctx, ctx_ref[...], x_ref[...])
    o = of_ref[...] + ob_ref[...]
    parts = []
    for h in range(GLA_HEADS):
        parts.append(_rms(o[:, h * dv:(h + 1) * dv]) * gn_ref[...])
    y = (jnp.concatenate(parts, axis=1) * r_ref[...].astype(F32)).astype(BF16)
    dx = jnp.dot(y, wo_ref[...], preferred_element_type=F32)
    out_ref[...] = xt + _mod_rows(modv_ref, is_ctx, 2, d) * dx


def _gla_readout(ctx2, x2, o_f, o_b, gr, modv, gn, wo):
    n_ctx, d = ctx2.shape
    n_all, dv_total = o_f.shape
    t = ROW_TILE
    row = lambda i: (i, 0)
    return pl.pallas_call(
        _gla_readout_kernel,
        grid=(n_all // t,),
        in_specs=[pl.BlockSpec((t, d), lambda i: (0, 0)),
                  pl.BlockSpec((t, d), lambda i: (jnp.maximum(i - 1, 0), 0)),
                  pl.BlockSpec((t, dv_total), row), pl.BlockSpec((t, dv_total), row),
                  pl.BlockSpec((t, dv_total), row),
                  _const_spec(modv.shape), _const_spec((1, gn.shape[0])), _const_spec(wo.shape)],
        out_specs=pl.BlockSpec((t, d), row),
        out_shape=jax.ShapeDtypeStruct((n_all, d), F32),
        compiler_params=_cparams("parallel"),
        name="gla_readout",
    )(ctx2, x2, o_f, o_b, gr, modv, gn.reshape(1, -1), wo.astype(BF16))


HALO = BF16_SUBLANES


def _ffn_kernel(prev_ref, x_ref, next_ref, modv_ref, g_ref, wup_ref, cw_ref, cb_ref, wdn_ref,
                fin_ref, out_ref, lhs_ref, u_ref, act_ref, *, ctx_tiles, final):
    t, d = x_ref.shape
    d_ff = wdn_ref.shape[0]
    i = pl.program_id(0)
    last = pl.num_programs(0) - 1
    is_ctx = i < ctx_tiles
    has_prev = jnp.logical_and(i != 0, i != ctx_tiles)
    has_next = jnp.logical_and(i != last, i != ctx_tiles - 1)
    g = g_ref[...]
    scale = _mod_rows(modv_ref, is_ctx, 4, d)
    shift = _mod_rows(modv_ref, is_ctx, 3, d)
    xt = x_ref[...]
    lhs_ref[0:HALO, :] = jnp.where(has_prev, _modnorm(prev_ref[...], g, scale, shift), 0.0).astype(BF16)
    lhs_ref[HALO:HALO + t, :] = _modnorm(xt, g, scale, shift).astype(BF16)
    lhs_ref[HALO + t:, :] = jnp.where(has_next, _modnorm(next_ref[...], g, scale, shift), 0.0).astype(BF16)
    lhs = lhs_ref[...]
    cc = FFN_COL_CHUNK
    for c in range(d_ff // cc):
        halves = []
        for half, off in ((0, c * cc), (1, d_ff + c * cc)):
            cols = slice(off, off + cc)
            u = u_ref.at[half, c]
            u[...] = jnp.dot(lhs, wup_ref[:, cols], preferred_element_type=F32)
            halves.append(u[HALO - 1:HALO - 1 + t, :] * cw_ref[0:1, cols]
                          + u[HALO:HALO + t, :] * cw_ref[1:2, cols]
                          + u[HALO + 1:HALO + 1 + t, :] * cw_ref[2:3, cols]
                          + cb_ref[:, cols])
        act_ref[:, c * cc:(c + 1) * cc] = (_silu(halves[0]) * halves[1]).astype(BF16)
    down = jnp.dot(act_ref[...], wdn_ref[...], preferred_element_type=F32)
    y = xt + _mod_rows(modv_ref, is_ctx, 5, d) * down
    if final:
        y = _rms(y) * fin_ref[...]
    out_ref[...] = y


def _conv_ffn(xs, modv, norm_g, w_up, w_conv, b_conv, w_down, final_g, *, ctx_tiles, final):
    n, d = xs.shape
    d_ff = w_down.shape[0]
    t = ROW_TILE
    assert d_ff % FFN_COL_CHUNK == 0
    per = t // HALO
    n_halo_blocks = n // HALO
    kern = functools.partial(_ffn_kernel, ctx_tiles=ctx_tiles, final=final)
    return pl.pallas_call(
        kern,
        grid=(n // t,),
        in_specs=[pl.BlockSpec((HALO, d), lambda i: (jnp.maximum(i * per - 1, 0), 0)),
                  pl.BlockSpec((t, d), lambda i: (i, 0)),
                  pl.BlockSpec((HALO, d), lambda i: (jnp.minimum((i + 1) * per, n_halo_blocks - 1), 0)),
                  _const_spec(modv.shape), _const_spec((1, d)),
                  _const_spec(w_up.shape), _const_spec(w_conv.shape), _const_spec((1, 2 * d_ff)),
                  _const_spec(w_down.shape), _const_spec((1, d))],
        out_specs=pl.BlockSpec((t, d), lambda i: (i, 0)),
        out_shape=jax.ShapeDtypeStruct((n, d), F32),
        scratch_shapes=[pltpu.VMEM((t + 2 * HALO, d), BF16),
                        pltpu.VMEM((2, d_ff // FFN_COL_CHUNK, t + 2 * HALO, FFN_COL_CHUNK), F32),
                        pltpu.VMEM((t, d_ff), BF16)],
        compiler_params=_cparams("parallel"),
        name="conv_ffn_final" if final else "conv_ffn",
    )(xs, xs, xs, modv, norm_g.reshape(1, d), w_up.astype(BF16), w_conv,
      b_conv.reshape(1, -1), w_down.astype(BF16), final_g.reshape(1, d))


def _rope_tables(n_tokens, width):
    rows = n_tokens // GRID_W
    row = jnp.repeat(jnp.arange(rows, dtype=F32), GRID_W)
    col = jnp.tile(jnp.arange(GRID_W, dtype=F32), rows)
    quarter = DIFF_HEAD_DIM // 4
    inv = ROPE_BASE ** (-jnp.arange(quarter, dtype=F32) / quarter)
    ang_r = row[:, None] * inv
    ang_c = col[:, None] * inv
    ang = jnp.concatenate([ang_r, ang_r, ang_c, ang_c], axis=-1)
    sign = jnp.tile(jnp.concatenate([-jnp.ones((quarter,), F32), jnp.ones((quarter,), F32)]), 2)
    reps = width // DIFF_HEAD_DIM
    return jnp.tile(jnp.cos(ang), (1, reps)), jnp.tile(jnp.sin(ang) * sign, (1, reps))


def _diff_proj_kernel(x_ref, cos_ref, sin_ref, modv_ref, g_ref, w_ref, qt_ref, k_ref, vt_ref):
    t, d = x_ref.shape
    is_ctx = pl.program_id(0) == 0
    h = _modnorm(x_ref[...], g_ref[...], _mod_rows(modv_ref, is_ctx, 1, d),
                 _mod_rows(modv_ref, is_ctx, 0, d)).astype(BF16)
    qkv = jnp.dot(h, w_ref[...], preferred_element_type=F32)
    n = qkv.shape[1] // 3
    quarter = DIFF_HEAD_DIM // 4
    reps = n // LANES
    cos = jnp.concatenate([cos_ref[...]] * reps, axis=1)
    sin = jnp.concatenate([sin_ref[...]] * reps, axis=1)
    lane = lax.broadcasted_iota(jnp.int32, (t, n), 1)
    first_half = (lane % (2 * quarter)) < quarter

    def rope(a):
        partner = jnp.where(first_half, pltpu.roll(a, n - quarter, 1), pltpu.roll(a, quarter, 1))
        return jnp.where(is_ctx, a, a * cos + partner * sin)

    q = rope(qkv[:, :n]) * (DIFF_HEAD_DIM ** -0.5 * math.log2(math.e))
    qt_ref[...] = q.T.astype(BF16)
    k = rope(qkv[:, n:2 * n])
    hw = 2 * DIFF_HEAD_DIM
    ones_col = jnp.where(lax.broadcasted_iota(jnp.int32, (t, ATT_KW - hw), 1) == 0, 1.0, 0.0)
    parts = []
    for hh in range(n // hw):
        parts += [k[:, hh * hw:(hh + 1) * hw], ones_col]
    k_ref[...] = jnp.concatenate(parts, axis=1).astype(BF16)
    vt_ref[...] = qkv[:, 2 * n:].T.astype(BF16)


def _diff_proj(xs, n_ctx, modv, norm_g, wq, wk, wv):
    n_all, d = xs.shape
    n_lat = n_all - n_ctx
    t = ROW_TILE
    n = wq.shape[1]
    w = jnp.concatenate([wq, wk, wv], axis=1).astype(BF16)
    cos, sin = _rope_tables(n_lat, LANES)
    kaug = n // (2 * DIFF_HEAD_DIM) * ATT_KW
    lat = lambda i: (jnp.maximum(i - 1, 0), 0)
    return pl.pallas_call(
        _diff_proj_kernel,
        grid=(n_all // t,),
        in_specs=[pl.BlockSpec((t, d), lambda i: (i, 0)),
                  pl.BlockSpec((t, LANES), lat), pl.BlockSpec((t, LANES), lat),
                  _const_spec(modv.shape), _const_spec((1, d)), _const_spec(w.shape)],
        out_specs=[pl.BlockSpec((n, t), lambda i: (0, jnp.maximum(i - 1, 0))),
                   pl.BlockSpec((t, kaug), lambda i: (i, 0)),
                   pl.BlockSpec((n, t), lambda i: (0, i))],
        out_shape=[jax.ShapeDtypeStruct((n, n_lat), BF16),
                   jax.ShapeDtypeStruct((n_all, kaug), BF16),
                   jax.ShapeDtypeStruct((n, n_all), BF16)],
        compiler_params=_cparams("arbitrary"),
        name="diff_proj",
    )(xs, cos, sin, modv, norm_g.reshape(1, d), w)


SHIFT_ROWS = BF16_SUBLANES


def _diff_attn_kernel(lam_ref, qt_ref, k_ref, vt_ref, o_ref, qa_ref, sh_ref, l_ref, acc_ref,
                      *, lambda_init):
    hd = DIFF_HEAD_DIM
    hw = 2 * hd
    tq = qt_ref.shape[1]
    kw = k_ref.shape[1]
    n_keys = k_ref.shape[0]
    tk = ATT_TK
    qt = qt_ref[...]
    rows = lax.broadcasted_iota(jnp.int32, qt.shape, 0)
    zero = jnp.zeros_like(qt)
    shift_row = lax.broadcasted_iota(jnp.int32, (SHIFT_ROWS, tq), 0) == 0

    def set_shift(br, value):
        sh_ref[br] = value
        qa_ref[br, hw:hw + SHIFT_ROWS, :] = jnp.where(shift_row, -value, 0.0).astype(BF16)

    def round_bf16(v):
        return v.astype(BF16).astype(F32)

    for br in range(2):
        keep = jnp.logical_and(rows >= br * hd, rows < (br + 1) * hd)
        qa_ref[br, 0:hw, :] = jnp.where(keep, qt, zero)
        qa_ref[br, hw:, :] = jnp.zeros((kw - hw, tq), BF16)
    l_ref[...] = jnp.zeros_like(l_ref)
    acc_ref[...] = jnp.zeros_like(acc_ref)
    k0 = k_ref[0:ATT_SUB, :]
    for br in range(2):
        s0 = jnp.dot(k0, qa_ref[br], preferred_element_type=F32)
        set_shift(br, round_bf16(jnp.max(s0, axis=0, keepdims=True)))

    def block(j):
        start = pl.multiple_of(j * tk, LANES)
        kb = k_ref[pl.ds(start, tk), :]
        vb = vt_ref[:, pl.ds(start, tk)]
        out = []
        for br in range(2):
            bm, cs, pv = [], [], []
            for sub in range(tq // ATT_SUB):
                cols = slice(sub * ATT_SUB, (sub + 1) * ATT_SUB)
                s = jnp.dot(kb, qa_ref[br, :, cols], preferred_element_type=F32)
                bm.append(jnp.max(s, axis=0, keepdims=True))
                p = jnp.exp2(s)
                cs.append(jnp.sum(p, axis=0, keepdims=True))
                pv.append(jnp.dot(vb, p.astype(BF16), preferred_element_type=F32))
            out += [jnp.concatenate(bm, axis=1), jnp.concatenate(cs, axis=1),
                    jnp.concatenate(pv, axis=1)]
        return tuple(out)

    def advance_shift(br, bm, extra_l, extra_acc):
        old = sh_ref[br]
        new = round_bf16(old + jnp.maximum(bm, 0.0))
        alpha = jnp.exp2(old - new)
        l_ref[br] = (l_ref[br] + extra_l) * alpha
        acc_ref[br] = (acc_ref[br] + extra_acc) * alpha
        set_shift(br, new)

    def body(j, carry):
        res = block(j)
        overflow = jnp.max(jnp.maximum(res[0], res[3])) > ATT_OVERFLOW_LOG2

        def redo():
            for br in range(2):
                advance_shift(br, res[3 * br], 0.0, 0.0)
            return block(j)

        res = lax.cond(overflow, redo, lambda: res)
        for br in range(2):
            advance_shift(br, res[3 * br], res[3 * br + 1], res[3 * br + 2])
        return carry

    lax.fori_loop(0, n_keys // tk, body, 0)
    lam = (jnp.exp(jnp.sum(lam_ref[0:1, :] * lam_ref[1:2, :], axis=1, keepdims=True))
           - jnp.exp(jnp.sum(lam_ref[2:3, :] * lam_ref[3:4, :], axis=1, keepdims=True))
           + lambda_init)
    o_ref[...] = acc_ref[0] / l_ref[0] - lam * (acc_ref[1] / l_ref[1])


def _diff_attn(qt, k, vt, lam_vecs, lambda_init):
    n, n_lat = qt.shape
    n_all = k.shape[0]
    hw = 2 * DIFF_HEAD_DIM
    heads = n // hw
    tq = ATT_TQ
    assert n_lat % tq == 0 and n_all % ATT_TK == 0 and tq % ATT_SUB == 0
    kern = functools.partial(_diff_attn_kernel, lambda_init=lambda_init)
    return pl.pallas_call(
        kern,
        grid=(heads, n_lat // tq),
        in_specs=[pl.BlockSpec(lam_vecs.shape, lambda h, i: (0, 0)),
                  pl.BlockSpec((hw, tq), lambda h, i: (h, i)),
                  pl.BlockSpec((n_all, ATT_KW), lambda h, i: (0, h)),
                  pl.BlockSpec((hw, n_all), lambda h, i: (h, 0))],
        out_specs=pl.BlockSpec((hw, tq), lambda h, i: (h, i)),
        out_shape=jax.ShapeDtypeStruct((n, n_lat), F32),
        scratch_shapes=[pltpu.VMEM((2, ATT_KW, tq), BF16), pltpu.VMEM((2, 1, tq), F32),
                        pltpu.VMEM((2, 1, tq), F32), pltpu.VMEM((2, hw, tq), F32)],
        compiler_params=_cparams("arbitrary", "arbitrary"),
        name="diff_attn",
    )(lam_vecs, qt, k, vt)


def _diff_readout_kernel(x_ref, ot_ref, modv_ref, sub_ref, wo_ref, out_ref, *, out_scale):
    d = x_ref.shape[1]
    hw = sub_ref.shape[0]
    ot = ot_ref[...]
    parts = []
    for h in range(ot.shape[0] // hw):
        oh = ot[h * hw:(h + 1) * hw, :]
        ms = jnp.mean(oh * oh, axis=0, keepdims=True)
        parts.append(oh * lax.rsqrt(ms + EPS) * sub_ref[...] * out_scale)
    y = jnp.concatenate(parts, axis=0).T.astype(BF16)
    dx = jnp.dot(y, wo_ref[...], preferred_element_type=F32)
    out_ref[...] = x_ref[...] + modv_ref[0:1, 2 * d:3 * d] * dx


def _diff_readout(xs, n_ctx, ot, modv, subln, wo, lambda_init):
    n, n_lat = ot.shape
    d = xs.shape[1]
    t = ROW_TILE
    ctx_tiles = n_ctx // t
    kern = functools.partial(_diff_readout_kernel, out_scale=1.0 - lambda_init)
    return pl.pallas_call(
        kern,
        grid=(n_lat // t,),
        in_specs=[pl.BlockSpec((t, d), lambda i: (i + ctx_tiles, 0)),
                  pl.BlockSpec((n, t), lambda i: (0, i)),
                  _const_spec(modv.shape), _const_spec((subln.shape[0], 1)), _const_spec(wo.shape)],
        out_specs=pl.BlockSpec((t, d), lambda i: (i, 0)),
        out_shape=jax.ShapeDtypeStruct((n_lat, d), F32),
        compiler_params=_cparams("parallel"),
        name="diff_readout",
    )(xs, ot, modv, subln.reshape(-1, 1), wo.astype(BF16))


def kernel(x, c, ctx, c_ctx, mod_w, mod_b, norm_mix, norm_ffn, gla_wq, gla_wk, gla_wv, gla_wr, gla_wg1, gla_wg2, gla_bg, gla_norm, gla_wo, diff_wq, diff_wk, diff_wv, diff_lq1, diff_lk1, diff_lq2, diff_lk2, diff_subln, diff_wo, ffn_wup, ffn_conv, ffn_conv_b, ffn_wdown, final_norm):
    bsz, n_lat, d = x.shape
    n_ctx = ctx.shape[1]
    depth = mod_w.shape[0]
    assert bsz == 1 and depth == 2, "layer 0 is the GLA mixer, layer 1 the differential attention"
    x2 = x.reshape(n_lat, d)
    ctx2 = ctx.reshape(n_ctx, d)
    cc = jnp.zeros((8, d), F32).at[0].set(c[0]).at[1].set(c_ctx)
    mod = _modulation(cc, mod_w, mod_b)

    q, k, v, gr, lg = _gla_proj(ctx2, x2, mod[0], norm_mix[0], gla_wq[0], gla_wk[0], gla_wv[0],
                                gla_wr[0], gla_wg1[0], gla_wg2[0], gla_bg[0])
    o_f, o_b = _gla_scan(q, k, v, lg, n_ctx)
    xs = _gla_readout(ctx2, x2, o_f, o_b, gr, mod[0], gla_norm[0], gla_wo[0])
    xs = _conv_ffn(xs, mod[0], norm_ffn[0], ffn_wup[0], ffn_conv[0], ffn_conv_b[0], ffn_wdown[0],
                   final_norm, ctx_tiles=n_ctx // ROW_TILE, final=False)

    lambda_init = 0.8 - 0.6 * math.exp(-0.3 * 1)
    qt, kk, vt = _diff_proj(xs, n_ctx, mod[1], norm_mix[1], diff_wq[0], diff_wk[0], diff_wv[0])
    lam_vecs = jnp.zeros((8, DIFF_HEAD_DIM), F32).at[0:4].set(
        jnp.concatenate([diff_lq1, diff_lk1, diff_lq2, diff_lk2], axis=0))
    ot = _diff_attn(qt, kk, vt, lam_vecs, lambda_init)
    xl = _diff_readout(xs, n_ctx, ot, mod[1], diff_subln[0], diff_wo[0], lambda_init)
    out = _conv_ffn(xl, mod[1], norm_ffn[1], ffn_wup[1], ffn_conv[1], ffn_conv_b[1], ffn_wdown[1],
                    final_norm, ctx_tiles=0, final=True)
    return out.reshape(bsz, n_lat, d)
```

```python
import functools
import math

import numpy as np
import jax
import jax.numpy as jnp
from jax import lax
from jax.experimental import pallas as pl
from jax.experimental.pallas import tpu as pltpu

GRID_W = 64
GLA_HEADS = 4
GLA_GATE_RANK = 16
GLA_GATE_NORM = 16.0
GLA_CHUNK = 64
DIFF_HEAD_DIM = 64
ROPE_BASE = 10000.0
CONV_WIDTH = 3
N_MOD = 6
EPS = 1e-6

LANES = 128
BF16_SUBLANES = 16
VMEM_LIMIT = 56 * 1024 * 1024

ROW_TILE = 256
FFN_COL_CHUNK = 256
ATT_TQ = 512
ATT_SUB = 512
ATT_TK = 3328
ATT_KW = 256
ATT_OVERFLOW_LOG2 = 64.0
MOD_COL_TILE = 1536

F32 = jnp.float32
BF16 = jnp.bfloat16

_NT = (((1,), (1,)), ((), ()))
_TN = (((0,), (0,)), ((), ()))


def _cparams(*sem):
    return pltpu.CompilerParams(dimension_semantics=sem, vmem_limit_bytes=VMEM_LIMIT)


def _const_spec(shape):
    nd = len(shape)
    return pl.BlockSpec(shape, lambda *_: (0,) * nd, pipeline_mode=pl.Buffered(1))


def _silu(x):
    return x * (1.0 / (1.0 + jnp.exp(-x)))


def _rms(x):
    return x * lax.rsqrt(jnp.mean(x * x, axis=-1, keepdims=True) + EPS)


def _mod_rows(modv_ref, is_ctx, k, d):
    sl = slice(k * d, (k + 1) * d)
    return jnp.where(is_ctx, modv_ref[1:2, sl], modv_ref[0:1, sl])


def _modnorm(x, g, scale, shift):
    return (_rms(x) * g) * (1.0 + scale) + shift


def _split2(x):
    hi = x.astype(BF16)
    lo = (x - hi.astype(F32)).astype(BF16)
    return hi, lo


def _mod_kernel(cc_ref, w_ref, b_ref, o_ref):
    s = _silu(cc_ref[...])
    w = w_ref[0]
    s_hi, s_lo = _split2(s)
    w_hi, w_lo = _split2(w)
    acc = jnp.dot(s_hi, w_hi, preferred_element_type=F32)
    acc += jnp.dot(s_lo, w_hi, preferred_element_type=F32)
    acc += jnp.dot(s_hi, w_lo, preferred_element_type=F32)
    o_ref[0] = acc + b_ref[0]


def _modulation(cc, mod_w, mod_b):
    depth, d, n = mod_w.shape
    tn = MOD_COL_TILE
    return pl.pallas_call(
        _mod_kernel,
        grid=(depth, n // tn),
        in_specs=[pl.BlockSpec((8, d), lambda l, j: (0, 0)),
                  pl.BlockSpec((1, d, tn), lambda l, j: (l, 0, j)),
                  pl.BlockSpec((1, 1, tn), lambda l, j: (l, 0, j))],
        out_specs=pl.BlockSpec((1, 8, tn), lambda l, j: (l, 0, j)),
        out_shape=jax.ShapeDtypeStruct((depth, 8, n), F32),
        compiler_params=_cparams("parallel", "parallel"),
        name="modulation",
    )(cc, mod_w, mod_b.reshape(depth, 1, n))


def _gla_proj_kernel(ctx_ref, x_ref, modv_ref, g_ref, w_ref, wg1_ref, wg2_ref, bg_ref,
                     q_ref, k_ref, v_ref, r_ref, lg_ref, *, dk_total, dv_total, q_scale):
    d = x_ref.shape[1]
    is_ctx = pl.program_id(0) == 0
    xt = jnp.where(is_ctx, ctx_ref[...], x_ref[...])
    h = _modnorm(xt, g_ref[...], _mod_rows(modv_ref, is_ctx, 1, d),
                 _mod_rows(modv_ref, is_ctx, 0, d)).astype(BF16)
    qkvr = jnp.dot(h, w_ref[...], preferred_element_type=F32)
    o1 = dk_total
    o2 = 2 * dk_total
    o3 = o2 + dv_total
    q_ref[...] = qkvr[:, :o1] * q_scale
    k_ref[...] = qkvr[:, o1:o2]
    v_ref[...] = qkvr[:, o2:o3].astype(BF16)
    r_ref[...] = _silu(qkvr[:, o3:]).astype(BF16)
    low = jnp.dot(h, wg1_ref[...], preferred_element_type=F32).astype(BF16)
    z = jnp.dot(low, wg2_ref[...], preferred_element_type=F32) + bg_ref[...]
    log_sig = jnp.minimum(z, 0.0) - jnp.log1p(jnp.exp(-jnp.abs(z)))
    lg_ref[...] = log_sig * (1.0 / GLA_GATE_NORM)


def _gla_proj(ctx2, x2, modv, norm_g, wq, wk, wv, wr, wg1, wg2, bg):
    n_ctx, d = ctx2.shape
    n_lat = x2.shape[0]
    t = ROW_TILE
    assert n_ctx == t and n_lat % t == 0
    n_all = n_ctx + n_lat
    dk_total = wq.shape[1]
    dv_total = wv.shape[1]
    rank = wg1.shape[2]
    w = jnp.concatenate([wq, wk, wv, wr], axis=1).astype(BF16)
    wg1c = jnp.zeros((d, LANES), F32).at[:, :rank].set(wg1[0]).at[:, rank:2 * rank].set(wg1[1])
    wg2c = (jnp.zeros((LANES, 2 * dk_total), F32)
            .at[:rank, :dk_total].set(wg2[0]).at[rank:2 * rank, dk_total:].set(wg2[1]))
    bgc = bg.reshape(1, 2 * dk_total)
    row = lambda i: (i, 0)
    kern = functools.partial(_gla_proj_kernel, dk_total=dk_total, dv_total=dv_total,
                             q_scale=(dk_total // GLA_HEADS) ** -0.5)
    return pl.pallas_call(
        kern,
        grid=(n_all // t,),
        in_specs=[pl.BlockSpec((t, d), lambda i: (0, 0)),
                  pl.BlockSpec((t, d), lambda i: (jnp.maximum(i - 1, 0), 0)),
                  _const_spec(modv.shape), _const_spec((1, d)),
                  _const_spec(w.shape), _const_spec(wg1c.shape), _const_spec(wg2c.shape),
                  _const_spec(bgc.shape)],
        out_specs=[pl.BlockSpec((t, dk_total), row), pl.BlockSpec((t, dk_total), row),
                   pl.BlockSpec((t, dv_total), row), pl.BlockSpec((t, dv_total), row),
                   pl.BlockSpec((t, 2 * dk_total), row)],
        out_shape=[jax.ShapeDtypeStruct((n_all, dk_total), F32),
                   jax.ShapeDtypeStruct((n_all, dk_total), F32),
                   jax.ShapeDtypeStruct((n_all, dv_total), BF16),
                   jax.ShapeDtypeStruct((n_all, dv_total), BF16),
                   jax.ShapeDtypeStruct((n_all, 2 * dk_total), F32)],
        compiler_params=_cparams("parallel"),
        name="gla_proj",
    )(ctx2, x2, modv, norm_g.reshape(1, d), w, wg1c.astype(BF16), wg2c.astype(BF16), bgc)


N_LEVELS = int(math.log2(GLA_CHUNK))
N_DECAY_BLOCKS = 2 + N_LEVELS


def _scan_tables():
    c = GLA_CHUNK
    t = np.arange(c)[:, None]
    u = np.arange(c)[None, :]
    blocks = [(u <= t), (u > t)]
    masks = [(t == u)]
    for lvl in range(1, N_LEVELS + 1):
        size = 1 << lvl
        half = size // 2
        start = (t // size) * size
        mid = start + half - 1
        upper = (t - start) >= half
        blocks.append((upper & (u > mid) & (u <= t)) | ((~upper) & (u > t) & (u <= mid)))
        same = (t // size) == (u // size)
        masks.append(same & upper & ((u - (u // size) * size) < half))
    fwd = np.concatenate(blocks, axis=0).astype(np.float32)
    fmask = np.concatenate(masks, axis=0).astype(np.float32)
    rev = lambda m: m.reshape(-1, c, c)[:, ::-1, ::-1].reshape(-1, c)
    twice = lambda m: np.concatenate([m, m], axis=1)
    return (np.stack([twice(fwd), twice(rev(fwd))]), np.stack([fmask, rev(fmask)]))


def _gla_scan_kernel(dmat_ref, mask_ref, qf, kf, vf, gf, qb, kb, vb, gb, of_ref, ob_ref, st_ref):
    @pl.when(pl.program_id(0) == 0)
    def _():
        st_ref[...] = jnp.zeros_like(st_ref)

    c = GLA_CHUNK
    n_sub = qf.shape[0] // c
    dk = qf.shape[1] // GLA_HEADS
    dv = vf.shape[1] // GLA_HEADS
    heads = range(GLA_HEADS)
    hsl = [slice(h * dk, (h + 1) * dk) for h in heads]
    vsl = [slice(h * dv, (h + 1) * dv) for h in heads]
    refs = ((qf, kf, vf, gf, of_ref), (qb, kb, vb, gb, ob_ref))
    work = [(d, ci) for step in range(n_sub) for d, ci in ((0, step), (1, n_sub - 1 - step))]
    rows = lambda ci: slice(ci * c, (ci + 1) * c)

    decay_x = {}
    for d, ci in work:
        g_hi, g_lo = _split2(refs[d][3][rows(ci), :])
        e = jnp.dot(dmat_ref[d], jnp.concatenate([g_hi, g_lo], axis=0),
                    preferred_element_type=F32)
        decay_x[d, ci] = jnp.exp(e)
    blk = lambda d, ci, n, h: decay_x[d, ci][n * c:(n + 1) * c, hsl[h]]

    scores = {}
    for d, ci in work:
        q_all = refs[d][0][rows(ci), :]
        k_all = refs[d][1][rows(ci), :]
        for h in heads:
            q = q_all[:, hsl[h]]
            k = k_all[:, hsl[h]]
            parts = [lax.dot_general(q.astype(BF16), k.astype(BF16), _NT,
                                     preferred_element_type=F32)]
            for lvl in range(1, N_LEVELS + 1):
                f = blk(d, ci, 1 + lvl, h)
                parts.append(lax.dot_general((q * f).astype(BF16), (k * f).astype(BF16), _NT,
                                             preferred_element_type=F32))
            scores[d, ci, h] = parts

    increments = {}
    intra = {}
    for d, ci in work:
        k_all = refs[d][1][rows(ci), :]
        v_all = refs[d][2][rows(ci), :]
        for h in heads:
            a = mask_ref[d, 0:c, :] * scores[d, ci, h][0]
            for lvl in range(1, N_LEVELS + 1):
                a += mask_ref[d, lvl * c:(lvl + 1) * c, :] * scores[d, ci, h][lvl]
            intra[d, ci, h] = jnp.dot(a.astype(BF16), v_all[:, vsl[h]], preferred_element_type=F32)
            k_end = (k_all[:, hsl[h]] * blk(d, ci, 1, h)).astype(BF16)
            increments[d, ci, h] = lax.dot_general(v_all[:, vsl[h]], k_end, _TN,
                                                   preferred_element_type=F32)

    for d, ci in work:
        last = c - 1 if d == 0 else 0
        q_all = refs[d][0][rows(ci), :]
        outs = []
        for h in heads:
            st = st_ref[d, h]
            inter = lax.dot_general((q_all[:, hsl[h]] * blk(d, ci, 0, h)).astype(BF16),
                                    st.astype(BF16), _NT, preferred_element_type=F32)
            outs.append(intra[d, ci, h] + inter)
            decay = decay_x[d, ci][last:last + 1, hsl[h]]
            st_ref[d, h] = st * decay + increments[d, ci, h]
        refs[d][4][rows(ci), :] = jnp.concatenate(outs, axis=1).astype(refs[d][4].dtype)


def _gla_scan(q, k, v, lg, n_ctx):
    n_all, dk_total = q.shape
    dv_total = v.shape[1]
    t = ROW_TILE
    assert t % GLA_CHUNK == 0 and n_ctx == t
    n_tiles = n_all // t
    dmat, masks = _scan_tables()
    fwd = lambda i: (i, 0)
    bidx = lambda i: jnp.where(i == 0, 0, n_tiles - i)
    bwd = lambda i: (bidx(i), 0)
    bwd_g = lambda i: (bidx(i), 1)
    return pl.pallas_call(
        _gla_scan_kernel,
        grid=(n_tiles,),
        in_specs=[_const_spec(dmat.shape), _const_spec(masks.shape),
                  pl.BlockSpec((t, dk_total), fwd), pl.BlockSpec((t, dk_total), fwd),
                  pl.BlockSpec((t, dv_total), fwd), pl.BlockSpec((t, dk_total), fwd),
                  pl.BlockSpec((t, dk_total), bwd), pl.BlockSpec((t, dk_total), bwd),
                  pl.BlockSpec((t, dv_total), bwd), pl.BlockSpec((t, dk_total), bwd_g)],
        out_specs=[pl.BlockSpec((t, dv_total), fwd), pl.BlockSpec((t, dv_total), bwd)],
        out_shape=[jax.ShapeDtypeStruct((n_all, dv_total), BF16),
                   jax.ShapeDtypeStruct((n_all, dv_total), BF16)],
        scratch_shapes=[pltpu.VMEM((2, GLA_HEADS, dv_total // GLA_HEADS, dk_total // GLA_HEADS), F32)],
        compiler_params=_cparams("arbitrary"),
        name="gla_scan",
    )(jnp.asarray(dmat, BF16), jnp.asarray(masks, F32), q, k, v, lg, q, k, v, lg)


def _gla_readout_kernel(ctx_ref, x_ref, of_ref, ob_ref, r_ref, modv_ref, gn_ref, wo_ref, out_ref):
    d = x_ref.shape[1]
    dv = gn_ref.shape[1]
    is_ctx = pl.program_id(0) == 0
    xt = jnp.where(is_ctx, ctx_ref[...], x_ref[...])
    o = of_ref[...].astype(F32) + ob_ref[...].astype(F32)
    parts = []
    for h in range(GLA_HEADS):
        parts.append(_rms(o[:, h * dv:(h + 1) * dv]) * gn_ref[...])
    y = (jnp.concatenate(parts, axis=1) * r_ref[...].astype(F32)).astype(BF16)
    dx = jnp.dot(y, wo_ref[...], preferred_element_type=F32)
    out_ref[...] = xt + _mod_rows(modv_ref, is_ctx, 2, d) * dx


def _gla_readout(ctx2, x2, o_f, o_b, gr, modv, gn, wo):
    n_ctx, d = ctx2.shape
    n_all, dv_total = o_f.shape
    t = ROW_TILE
    row = lambda i: (i, 0)
    return pl.pallas_call(
        _gla_readout_kernel,
        grid=(n_all // t,),
        in_specs=[pl.BlockSpec((t, d), lambda i: (0, 0)),
                  pl.BlockSpec((t, d), lambda i: (jnp.maximum(i - 1, 0), 0)),
                  pl.BlockSpec((t, dv_total), row), pl.BlockSpec((t, dv_total), row),
                  pl.BlockSpec((t, dv_total), row),
                  _const_spec(modv.shape), _const_spec((1, gn.shape[0])), _const_spec(wo.shape)],
        out_specs=pl.BlockSpec((t, d), row),
        out_shape=jax.ShapeDtypeStruct((n_all, d), F32),
        compiler_params=_cparams("parallel"),
        name="gla_readout",
    )(ctx2, x2, o_f, o_b, gr, modv, gn.reshape(1, -1), wo.astype(BF16))


HALO = BF16_SUBLANES


def _ffn_kernel(prev_ref, x_ref, next_ref, modv_ref, g_ref, wup_ref, cw_ref, cb_ref, wdn_ref,
                fin_ref, out_ref, lhs_ref, u_ref, act_ref, *, ctx_tiles, final):
    t, d = x_ref.shape
    d_ff = wdn_ref.shape[0]
    i = pl.program_id(0)
    last = pl.num_programs(0) - 1
    is_ctx = i < ctx_tiles
    has_prev = jnp.logical_and(i != 0, i != ctx_tiles)
    has_next = jnp.logical_and(i != last, i != ctx_tiles - 1)
    g = g_ref[...]
    scale = _mod_rows(modv_ref, is_ctx, 4, d)
    shift = _mod_rows(modv_ref, is_ctx, 3, d)
    xt = x_ref[...]
    lhs_ref[0:HALO, :] = jnp.where(has_prev, _modnorm(prev_ref[...], g, scale, shift), 0.0).astype(BF16)
    lhs_ref[HALO:HALO + t, :] = _modnorm(xt, g, scale, shift).astype(BF16)
    lhs_ref[HALO + t:, :] = jnp.where(has_next, _modnorm(next_ref[...], g, scale, shift), 0.0).astype(BF16)
    lhs = lhs_ref[...]
    cc = FFN_COL_CHUNK
    for c in range(d_ff // cc):
        halves = []
        for half, off in ((0, c * cc), (1, d_ff + c * cc)):
            cols = slice(off, off + cc)
            u = u_ref.at[half, c]
            u[...] = jnp.dot(lhs, wup_ref[:, cols], preferred_element_type=F32)
            halves.append(u[HALO - 1:HALO - 1 + t, :] * cw_ref[0:1, cols]
                          + u[HALO:HALO + t, :] * cw_ref[1:2, cols]
                          + u[HALO + 1:HALO + 1 + t, :] * cw_ref[2:3, cols]
                          + cb_ref[:, cols])
        act_ref[:, c * cc:(c + 1) * cc] = (_silu(halves[0]) * halves[1]).astype(BF16)
    down = jnp.dot(act_ref[...], wdn_ref[...], preferred_element_type=F32)
    y = xt + _mod_rows(modv_ref, is_ctx, 5, d) * down
    if final:
        y = _rms(y) * fin_ref[...]
    out_ref[...] = y


def _conv_ffn(xs, modv, norm_g, w_up, w_conv, b_conv, w_down, final_g, *, ctx_tiles, final):
    n, d = xs.shape
    d_ff = w_down.shape[0]
    t = ROW_TILE
    assert d_ff % FFN_COL_CHUNK == 0
    per = t // HALO
    n_halo_blocks = n // HALO
    kern = functools.partial(_ffn_kernel, ctx_tiles=ctx_tiles, final=final)
    return pl.pallas_call(
        kern,
        grid=(n // t,),
        in_specs=[pl.BlockSpec((HALO, d), lambda i: (jnp.maximum(i * per - 1, 0), 0)),
                  pl.BlockSpec((t, d), lambda i: (i, 0)),
                  pl.BlockSpec((HALO, d), lambda i: (jnp.minimum((i + 1) * per, n_halo_blocks - 1), 0)),
                  _const_spec(modv.shape), _const_spec((1, d)),
                  _const_spec(w_up.shape), _const_spec(w_conv.shape), _const_spec((1, 2 * d_ff)),
                  _const_spec(w_down.shape), _const_spec((1, d))],
        out_specs=pl.BlockSpec((t, d), lambda i: (i, 0)),
        out_shape=jax.ShapeDtypeStruct((n, d), F32),
        scratch_shapes=[pltpu.VMEM((t + 2 * HALO, d), BF16),
                        pltpu.VMEM((2, d_ff // FFN_COL_CHUNK, t + 2 * HALO, FFN_COL_CHUNK), F32),
                        pltpu.VMEM((t, d_ff), BF16)],
        compiler_params=_cparams("parallel"),
        name="conv_ffn_final" if final else "conv_ffn",
    )(xs, xs, xs, modv, norm_g.reshape(1, d), w_up.astype(BF16), w_conv,
      b_conv.reshape(1, -1), w_down.astype(BF16), final_g.reshape(1, d))


def _rope_tables(n_tokens, width):
    rows = n_tokens // GRID_W
    row = jnp.repeat(jnp.arange(rows, dtype=F32), GRID_W)
    col = jnp.tile(jnp.arange(GRID_W, dtype=F32), rows)
    quarter = DIFF_HEAD_DIM // 4
    inv = ROPE_BASE ** (-jnp.arange(quarter, dtype=F32) / quarter)
    ang_r = row[:, None] * inv
    ang_c = col[:, None] * inv
    ang = jnp.concatenate([ang_r, ang_r, ang_c, ang_c], axis=-1)
    sign = jnp.tile(jnp.concatenate([-jnp.ones((quarter,), F32), jnp.ones((quarter,), F32)]), 2)
    reps = width // DIFF_HEAD_DIM
    return jnp.tile(jnp.cos(ang), (1, reps)), jnp.tile(jnp.sin(ang) * sign, (1, reps))


def _diff_proj_kernel(x_ref, cos_ref, sin_ref, modv_ref, g_ref, w_ref, qt_ref, k_ref, vt_ref):
    t, d = x_ref.shape
    is_ctx = pl.program_id(0) == 0
    h = _modnorm(x_ref[...], g_ref[...], _mod_rows(modv_ref, is_ctx, 1, d),
                 _mod_rows(modv_ref, is_ctx, 0, d)).astype(BF16)
    qkv = jnp.dot(h, w_ref[...], preferred_element_type=F32)
    n = qkv.shape[1] // 3
    quarter = DIFF_HEAD_DIM // 4
    reps = n // LANES
    cos = jnp.concatenate([cos_ref[...]] * reps, axis=1)
    sin = jnp.concatenate([sin_ref[...]] * reps, axis=1)
    lane = lax.broadcasted_iota(jnp.int32, (t, n), 1)
    first_half = (lane % (2 * quarter)) < quarter

    def rope(a):
        partner = jnp.where(first_half, pltpu.roll(a, n - quarter, 1), pltpu.roll(a, quarter, 1))
        return jnp.where(is_ctx, a, a * cos + partner * sin)

    q = rope(qkv[:, :n]) * (DIFF_HEAD_DIM ** -0.5 * math.log2(math.e))
    qt_ref[...] = q.T.astype(BF16)
    k = rope(qkv[:, n:2 * n])
    hw = 2 * DIFF_HEAD_DIM
    ones_col = jnp.where(lax.broadcasted_iota(jnp.int32, (t, ATT_KW - hw), 1) < SHIFT_PARTS, 1.0, 0.0)
    parts = []
    for hh in range(n // hw):
        parts += [k[:, hh * hw:(hh + 1) * hw], ones_col]
    k_ref[...] = jnp.concatenate(parts, axis=1).astype(BF16)
    vt_ref[...] = qkv[:, 2 * n:].T.astype(BF16)


def _diff_proj(xs, n_ctx, modv, norm_g, wq, wk, wv):
    n_all, d = xs.shape
    n_lat = n_all - n_ctx
    t = ROW_TILE
    n = wq.shape[1]
    w = jnp.concatenate([wq, wk, wv], axis=1).astype(BF16)
    cos, sin = _rope_tables(n_lat, LANES)
    kaug = n // (2 * DIFF_HEAD_DIM) * ATT_KW
    lat = lambda i: (jnp.maximum(i - 1, 0), 0)
    return pl.pallas_call(
        _diff_proj_kernel,
        grid=(n_all // t,),
        in_specs=[pl.BlockSpec((t, d), lambda i: (i, 0)),
                  pl.BlockSpec((t, LANES), lat), pl.BlockSpec((t, LANES), lat),
                  _const_spec(modv.shape), _const_spec((1, d)), _const_spec(w.shape)],
        out_specs=[pl.BlockSpec((n, t), lambda i: (0, jnp.maximum(i - 1, 0))),
                   pl.BlockSpec((t, kaug), lambda i: (i, 0)),
                   pl.BlockSpec((n, t), lambda i: (0, i))],
        out_shape=[jax.ShapeDtypeStruct((n, n_lat), BF16),
                   jax.ShapeDtypeStruct((n_all, kaug), BF16),
                   jax.ShapeDtypeStruct((n, n_all), BF16)],
        compiler_params=_cparams("arbitrary"),
        name="diff_proj",
    )(xs, cos, sin, modv, norm_g.reshape(1, d), w)


SHIFT_ROWS = BF16_SUBLANES
SHIFT_PARTS = 2


def _diff_attn_kernel(lam_ref, qt_ref, k_ref, vt_ref, o_ref, qa_ref, sh_ref, l_ref, acc_ref,
                      far_ref, *, lambda_init):
    hd = DIFF_HEAD_DIM
    hw = 2 * hd
    tq = qt_ref.shape[1]
    kw = k_ref.shape[1]
    n_blocks = k_ref.shape[0] // ATT_TK
    tk = ATT_TK
    qt = qt_ref[...]
    rows = lax.broadcasted_iota(jnp.int32, qt.shape, 0)
    zero = jnp.zeros_like(qt)
    shift_row = lax.broadcasted_iota(jnp.int32, (SHIFT_ROWS, tq), 0)

    def set_shift(br, value):
        hi, lo = _split2(value)
        sh_ref[br] = value
        qa_ref[br, hw:hw + SHIFT_ROWS, :] = jnp.where(
            shift_row == 0, -hi.astype(F32),
            jnp.where(shift_row == 1, -lo.astype(F32), 0.0)).astype(BF16)

    def round_shift(v):
        hi, lo = _split2(v)
        return hi.astype(F32) + lo.astype(F32)

    def reset_sums():
        l_ref[...] = jnp.zeros_like(l_ref)
        acc_ref[...] = jnp.zeros_like(acc_ref)

    def key_block(j):
        start = pl.multiple_of(j * tk, LANES)
        return k_ref[pl.ds(start, tk), :], vt_ref[:, pl.ds(start, tk)]

    def scores(kb, br, sub):
        cols = slice(sub * ATT_SUB, (sub + 1) * ATT_SUB)
        return jnp.dot(kb, qa_ref[br, :, cols], preferred_element_type=F32)

    def block(j, br):
        kb, vb = key_block(j)
        bm, cs, pv = [], [], []
        for sub in range(tq // ATT_SUB):
            s = scores(kb, br, sub)
            bm.append(jnp.max(s, axis=0, keepdims=True))
            p = jnp.exp2(s)
            cs.append(jnp.sum(p, axis=0, keepdims=True))
            pv.append(jnp.dot(vb, p.astype(BF16), preferred_element_type=F32))
        return (jnp.concatenate(bm, axis=1), jnp.concatenate(cs, axis=1),
                jnp.concatenate(pv, axis=1))

    for br in range(2):
        keep = jnp.logical_and(rows >= br * hd, rows < (br + 1) * hd)
        qa_ref[br, 0:hw, :] = jnp.where(keep, qt, zero)
        qa_ref[br, hw:, :] = jnp.zeros((kw - hw, tq), BF16)
        sh_ref[br] = jnp.zeros((1, tq), F32)
    reset_sums()
    far_ref[...] = jnp.zeros_like(far_ref)

    def body(j, carry):
        first = j == 0
        for br in range(2):
            bm, cs, pv = block(j, br)
            move = jnp.where(first, bm, jnp.maximum(bm, 0.0))
            far_ref[br] = jnp.maximum(far_ref[br], jnp.where(first, jnp.abs(bm), bm))
            old = sh_ref[br]
            new = round_shift(old + move)
            alpha = jnp.exp2(old - new)
            l_ref[br] = (l_ref[br] + cs) * alpha
            acc_ref[br] = (acc_ref[br] + pv) * alpha
            set_shift(br, new)
        return carry

    lax.fori_loop(0, n_blocks, body, 0)

    @pl.when(jnp.max(far_ref[...]) > ATT_OVERFLOW_LOG2)
    def _():
        for br in range(2):
            set_shift(br, jnp.zeros((1, tq), F32))

        def max_body(j, m):
            kb, _ = key_block(j)
            out = []
            for br in range(2):
                cols = [jnp.max(scores(kb, br, sub), axis=0, keepdims=True)
                        for sub in range(tq // ATT_SUB)]
                out.append(jnp.maximum(m[br], jnp.concatenate(cols, axis=1)))
            return tuple(out)

        neg = jnp.full((1, tq), -jnp.inf, F32)
        exact_max = lax.fori_loop(0, n_blocks, max_body, (neg, neg))
        for br in range(2):
            set_shift(br, round_shift(exact_max[br]))
        reset_sums()

        def sum_body(j, carry):
            for br in range(2):
                _, cs, pv = block(j, br)
                l_ref[br] += cs
                acc_ref[br] += pv
            return carry

        lax.fori_loop(0, n_blocks, sum_body, 0)

    lam = (jnp.exp(jnp.sum(lam_ref[0:1, :] * lam_ref[1:2, :], axis=1, keepdims=True))
           - jnp.exp(jnp.sum(lam_ref[2:3, :] * lam_ref[3:4, :], axis=1, keepdims=True))
           + lambda_init)
    o_ref[...] = (acc_ref[0] / l_ref[0] - lam * (acc_ref[1] / l_ref[1])).astype(o_ref.dtype)


def _diff_attn(qt, k, vt, lam_vecs, lambda_init):
    n, n_lat = qt.shape
    n_all = k.shape[0]
    hw = 2 * DIFF_HEAD_DIM
    heads = n // hw
    tq = ATT_TQ
    assert n_lat % tq == 0 and n_all % ATT_TK == 0 and tq % ATT_SUB == 0
    kern = functools.partial(_diff_attn_kernel, lambda_init=lambda_init)
    return pl.pallas_call(
        kern,
        grid=(heads, n_lat // tq),
        in_specs=[pl.BlockSpec(lam_vecs.shape, lambda h, i: (0, 0)),
                  pl.BlockSpec((hw, tq), lambda h, i: (h, i)),
                  pl.BlockSpec((n_all, ATT_KW), lambda h, i: (0, h)),
                  pl.BlockSpec((hw, n_all), lambda h, i: (h, 0))],
        out_specs=pl.BlockSpec((hw, tq), lambda h, i: (h, i)),
        out_shape=jax.ShapeDtypeStruct((n, n_lat), BF16),
        scratch_shapes=[pltpu.VMEM((2, ATT_KW, tq), BF16), pltpu.VMEM((2, 1, tq), F32),
                        pltpu.VMEM((2, 1, tq), F32), pltpu.VMEM((2, hw, tq), F32),
                        pltpu.VMEM((2, 1, tq), F32)],
        compiler_params=_cparams("arbitrary", "arbitrary"),
        name="diff_attn",
    )(lam_vecs, qt, k, vt)


def _diff_readout_kernel(x_ref, ot_ref, modv_ref, sub_ref, wo_ref, out_ref, *, out_scale):
    d = x_ref.shape[1]
    hw = sub_ref.shape[0]
    ot = ot_ref[...].astype(F32)
    parts = []
    for h in range(ot.shape[0] // hw):
        oh = ot[h * hw:(h + 1) * hw, :]
        ms = jnp.mean(oh * oh, axis=0, keepdims=True)
        parts.append(oh * lax.rsqrt(ms + EPS) * sub_ref[...] * out_scale)
    y = jnp.concatenate(parts, axis=0).T.astype(BF16)
    dx = jnp.dot(y, wo_ref[...], preferred_element_type=F32)
    out_ref[...] = x_ref[...] + modv_ref[0:1, 2 * d:3 * d] * dx


def _diff_readout(xs, n_ctx, ot, modv, subln, wo, lambda_init):
    n, n_lat = ot.shape
    d = xs.shape[1]
    t = ROW_TILE
    ctx_tiles = n_ctx // t
    kern = functools.partial(_diff_readout_kernel, out_scale=1.0 - lambda_init)
    return pl.pallas_call(
        kern,
        grid=(n_lat // t,),
        in_specs=[pl.BlockSpec((t, d), lambda i: (i + ctx_tiles, 0)),
                  pl.BlockSpec((n, t), lambda i: (0, i)),
                  _const_spec(modv.shape), _const_spec((subln.shape[0], 1)), _const_spec(wo.shape)],
        out_specs=pl.BlockSpec((t, d), lambda i: (i, 0)),
        out_shape=jax.ShapeDtypeStruct((n_lat, d), F32),
        compiler_params=_cparams("parallel"),
        name="diff_readout",
    )(xs, ot, modv, subln.reshape(-1, 1), wo.astype(BF16))


def kernel(x, c, ctx, c_ctx, mod_w, mod_b, norm_mix, norm_ffn, gla_wq, gla_wk, gla_wv, gla_wr, gla_wg1, gla_wg2, gla_bg, gla_norm, gla_wo, diff_wq, diff_wk, diff_wv, diff_lq1, diff_lk1, diff_lq2, diff_lk2, diff_subln, diff_wo, ffn_wup, ffn_conv, ffn_conv_b, ffn_wdown, final_norm):
    bsz, n_lat, d = x.shape
    n_ctx = ctx.shape[1]
    depth = mod_w.shape[0]
    assert bsz == 1 and depth == 2, "layer 0 is the GLA mixer, layer 1 the differential attention"
    x2 = x.reshape(n_lat, d)
    ctx2 = ctx.reshape(n_ctx, d)
    cc = jnp.zeros((8, d), F32).at[0].set(c[0]).at[1].set(c_ctx)
    mod = _modulation(cc, mod_w, mod_b)

    q, k, v, gr, lg = _gla_proj(ctx2, x2, mod[0], norm_mix[0], gla_wq[0], gla_wk[0], gla_wv[0],
                                gla_wr[0], gla_wg1[0], gla_wg2[0], gla_bg[0])
    o_f, o_b = _gla_scan(q, k, v, lg, n_ctx)
    xs = _gla_readout(ctx2, x2, o_f, o_b, gr, mod[0], gla_norm[0], gla_wo[0])
    xs = _conv_ffn(xs, mod[0], norm_ffn[0], ffn_wup[0], ffn_conv[0], ffn_conv_b[0], ffn_wdown[0],
                   final_norm, ctx_tiles=n_ctx // ROW_TILE, final=False)

    lambda_init = 0.8 - 0.6 * math.exp(-0.3 * 1)
    qt, kk, vt = _diff_proj(xs, n_ctx, mod[1], norm_mix[1], diff_wq[0], diff_wk[0], diff_wv[0])
    lam_vecs = jnp.zeros((8, DIFF_HEAD_DIM), F32).at[0:4].set(
        jnp.concatenate([diff_lq1, diff_lk1, diff_lq2, diff_lk2], axis=0))
    ot = _diff_attn(qt, kk, vt, lam_vecs, lambda_init)
    xl = _diff_readout(xs, n_ctx, ot, mod[1], diff_subln[0], diff_wo[0], lambda_init)
    out = _conv_ffn(xl, mod[1], norm_ffn[1], ffn_wup[1], ffn_conv[1], ffn_conv_b[1], ffn_wdown[1],
                    final_norm, ctx_tiles=0, final=True)
    return out.reshape(bsz, n_lat, d)
```

```python
import functools
import math

import numpy as np
import jax
import jax.numpy as jnp
from jax import lax
from jax.experimental import pallas as pl
from jax.experimental.pallas import tpu as pltpu

GRID_W = 64
GLA_HEADS = 4
GLA_GATE_RANK = 16
GLA_GATE_NORM = 16.0
GLA_CHUNK = 64
DIFF_HEAD_DIM = 64
ROPE_BASE = 10000.0
CONV_WIDTH = 3
N_MOD = 6
EPS = 1e-6

LANES = 128
BF16_SUBLANES = 16
VMEM_LIMIT = 56 * 1024 * 1024

ROW_TILE = 256
FFN_COL_CHUNK = 256
ATT_TQ = 512
ATT_SUB = 512
ATT_TK = 3328
ATT_KW = 256
ATT_OVERFLOW_LOG2 = 64.0
MOD_COL_TILE = 1536

F32 = jnp.float32
BF16 = jnp.bfloat16

_NT = (((1,), (1,)), ((), ()))
_TN = (((0,), (0,)), ((), ()))


def _cparams(*sem):
    return pltpu.CompilerParams(dimension_semantics=sem, vmem_limit_bytes=VMEM_LIMIT)


def _const_spec(shape):
    nd = len(shape)
    return pl.BlockSpec(shape, lambda *_: (0,) * nd, pipeline_mode=pl.Buffered(1))


def _silu(x):
    return x * (1.0 / (1.0 + jnp.exp(-x)))


def _rms(x):
    return x * lax.rsqrt(jnp.mean(x * x, axis=-1, keepdims=True) + EPS)


def _mod_rows(modv_ref, is_ctx, k, d):
    sl = slice(k * d, (k + 1) * d)
    return jnp.where(is_ctx, modv_ref[1:2, sl], modv_ref[0:1, sl])


def _modnorm(x, g, scale, shift):
    return (_rms(x) * g) * (1.0 + scale) + shift


def _split2(x):
    hi = x.astype(BF16)
    lo = (x - hi.astype(F32)).astype(BF16)
    return hi, lo


def _mod_kernel(cc_ref, w_ref, b_ref, o_ref):
    s = _silu(cc_ref[...])
    w = w_ref[0]
    s_hi, s_lo = _split2(s)
    w_hi, w_lo = _split2(w)
    acc = jnp.dot(s_hi, w_hi, preferred_element_type=F32)
    acc += jnp.dot(s_lo, w_hi, preferred_element_type=F32)
    acc += jnp.dot(s_hi, w_lo, preferred_element_type=F32)
    o_ref[0] = acc + b_ref[0]


def _modulation(cc, mod_w, mod_b):
    depth, d, n = mod_w.shape
    tn = MOD_COL_TILE
    return pl.pallas_call(
        _mod_kernel,
        grid=(depth, n // tn),
        in_specs=[pl.BlockSpec((8, d), lambda l, j: (0, 0)),
                  pl.BlockSpec((1, d, tn), lambda l, j: (l, 0, j)),
                  pl.BlockSpec((1, 1, tn), lambda l, j: (l, 0, j))],
        out_specs=pl.BlockSpec((1, 8, tn), lambda l, j: (l, 0, j)),
        out_shape=jax.ShapeDtypeStruct((depth, 8, n), F32),
        compiler_params=_cparams("parallel", "parallel"),
        name="modulation",
    )(cc, mod_w, mod_b.reshape(depth, 1, n))


def _gla_proj_kernel(ctx_ref, x_ref, modv_ref, g_ref, w_ref, wg1_ref, wg2_ref, bg_ref,
                     q_ref, k_ref, v_ref, r_ref, lg_ref, *, dk_total, dv_total, q_scale):
    d = x_ref.shape[1]
    is_ctx = pl.program_id(0) == 0
    xt = jnp.where(is_ctx, ctx_ref[...], x_ref[...])
    h = _modnorm(xt, g_ref[...], _mod_rows(modv_ref, is_ctx, 1, d),
                 _mod_rows(modv_ref, is_ctx, 0, d)).astype(BF16)
    qkvr = jnp.dot(h, w_ref[...], preferred_element_type=F32)
    o1 = dk_total
    o2 = 2 * dk_total
    o3 = o2 + dv_total
    q_ref[...] = qkvr[:, :o1] * q_scale
    k_ref[...] = qkvr[:, o1:o2]
    v_ref[...] = qkvr[:, o2:o3].astype(BF16)
    r_ref[...] = _silu(qkvr[:, o3:]).astype(BF16)
    low = jnp.dot(h, wg1_ref[...], preferred_element_type=F32).astype(BF16)
    z = jnp.dot(low, wg2_ref[...], preferred_element_type=F32) + bg_ref[...]
    log_sig = jnp.minimum(z, 0.0) - jnp.log1p(jnp.exp(-jnp.abs(z)))
    lg_ref[...] = log_sig * (1.0 / GLA_GATE_NORM)


def _gla_proj(ctx2, x2, modv, norm_g, wq, wk, wv, wr, wg1, wg2, bg):
    n_ctx, d = ctx2.shape
    n_lat = x2.shape[0]
    t = ROW_TILE
    assert n_ctx == t and n_lat % t == 0
    n_all = n_ctx + n_lat
    dk_total = wq.shape[1]
    dv_total = wv.shape[1]
    rank = wg1.shape[2]
    w = jnp.concatenate([wq, wk, wv, wr], axis=1).astype(BF16)
    wg1c = jnp.zeros((d, LANES), F32).at[:, :rank].set(wg1[0]).at[:, rank:2 * rank].set(wg1[1])
    wg2c = (jnp.zeros((LANES, 2 * dk_total), F32)
            .at[:rank, :dk_total].set(wg2[0]).at[rank:2 * rank, dk_total:].set(wg2[1]))
    bgc = bg.reshape(1, 2 * dk_total)
    row = lambda i: (i, 0)
    kern = functools.partial(_gla_proj_kernel, dk_total=dk_total, dv_total=dv_total,
                             q_scale=(dk_total // GLA_HEADS) ** -0.5)
    return pl.pallas_call(
        kern,
        grid=(n_all // t,),
        in_specs=[pl.BlockSpec((t, d), lambda i: (0, 0)),
                  pl.BlockSpec((t, d), lambda i: (jnp.maximum(i - 1, 0), 0)),
                  _const_spec(modv.shape), _const_spec((1, d)),
                  _const_spec(w.shape), _const_spec(wg1c.shape), _const_spec(wg2c.shape),
                  _const_spec(bgc.shape)],
        out_specs=[pl.BlockSpec((t, dk_total), row), pl.BlockSpec((t, dk_total), row),
                   pl.BlockSpec((t, dv_total), row), pl.BlockSpec((t, dv_total), row),
                   pl.BlockSpec((t, 2 * dk_total), row)],
        out_shape=[jax.ShapeDtypeStruct((n_all, dk_total), F32),
                   jax.ShapeDtypeStruct((n_all, dk_total), F32),
                   jax.ShapeDtypeStruct((n_all, dv_total), BF16),
                   jax.ShapeDtypeStruct((n_all, dv_total), BF16),
                   jax.ShapeDtypeStruct((n_all, 2 * dk_total), F32)],
        compiler_params=_cparams("parallel"),
        name="gla_proj",
    )(ctx2, x2, modv, norm_g.reshape(1, d), w, wg1c.astype(BF16), wg2c.astype(BF16), bgc)


N_LEVELS = int(math.log2(GLA_CHUNK))
N_DECAY_BLOCKS = 2 + N_LEVELS


def _scan_tables():
    c = GLA_CHUNK
    t = np.arange(c)[:, None]
    u = np.arange(c)[None, :]
    blocks = [(u <= t), (u > t)]
    masks = [(t == u)]
    for lvl in range(1, N_LEVELS + 1):
        size = 1 << lvl
        half = size // 2
        start = (t // size) * size
        mid = start + half - 1
        upper = (t - start) >= half
        blocks.append((upper & (u > mid) & (u <= t)) | ((~upper) & (u > t) & (u <= mid)))
        same = (t // size) == (u // size)
        masks.append(same & upper & ((u - (u // size) * size) < half))
    fwd = np.concatenate(blocks, axis=0).astype(np.float32)
    fmask = np.concatenate(masks, axis=0).astype(np.float32)
    rev = lambda m: m.reshape(-1, c, c)[:, ::-1, ::-1].reshape(-1, c)
    twice = lambda m: np.concatenate([m, m], axis=1)
    return (np.stack([twice(fwd), twice(rev(fwd))]), np.stack([fmask, rev(fmask)]))


def _gla_scan_kernel(dmat_ref, mask_ref, qf, kf, vf, gf, qb, kb, vb, gb, of_ref, ob_ref, st_ref):
    @pl.when(pl.program_id(0) == 0)
    def _():
        st_ref[...] = jnp.zeros_like(st_ref)

    c = GLA_CHUNK
    n_sub = qf.shape[0] // c
    dk = qf.shape[1] // GLA_HEADS
    dv = vf.shape[1] // GLA_HEADS
    heads = range(GLA_HEADS)
    hsl = [slice(h * dk, (h + 1) * dk) for h in heads]
    vsl = [slice(h * dv, (h + 1) * dv) for h in heads]
    refs = ((qf, kf, vf, gf, of_ref), (qb, kb, vb, gb, ob_ref))
    work = [(d, ci) for step in range(n_sub) for d, ci in ((0, step), (1, n_sub - 1 - step))]
    rows = lambda ci: slice(ci * c, (ci + 1) * c)

    decay_x = {}
    for d, ci in work:
        g_hi, g_lo = _split2(refs[d][3][rows(ci), :])
        e = jnp.dot(dmat_ref[d], jnp.concatenate([g_hi, g_lo], axis=0),
                    preferred_element_type=F32)
        decay_x[d, ci] = jnp.exp(e)
    blk = lambda d, ci, n, h: decay_x[d, ci][n * c:(n + 1) * c, hsl[h]]

    scores = {}
    for d, ci in work:
        q_all = refs[d][0][rows(ci), :]
        k_all = refs[d][1][rows(ci), :]
        for h in heads:
            q = q_all[:, hsl[h]]
            k = k_all[:, hsl[h]]
            parts = [lax.dot_general(q.astype(BF16), k.astype(BF16), _NT,
                                     preferred_element_type=F32)]
            for lvl in range(1, N_LEVELS + 1):
                f = blk(d, ci, 1 + lvl, h)
                parts.append(lax.dot_general((q * f).astype(BF16), (k * f).astype(BF16), _NT,
                                             preferred_element_type=F32))
            scores[d, ci, h] = parts

    increments = {}
    intra = {}
    for d, ci in work:
        k_all = refs[d][1][rows(ci), :]
        v_all = refs[d][2][rows(ci), :]
        for h in heads:
            a = mask_ref[d, 0:c, :] * scores[d, ci, h][0]
            for lvl in range(1, N_LEVELS + 1):
                a += mask_ref[d, lvl * c:(lvl + 1) * c, :] * scores[d, ci, h][lvl]
            intra[d, ci, h] = jnp.dot(a.astype(BF16), v_all[:, vsl[h]], preferred_element_type=F32)
            k_end = (k_all[:, hsl[h]] * blk(d, ci, 1, h)).astype(BF16)
            increments[d, ci, h] = lax.dot_general(v_all[:, vsl[h]], k_end, _TN,
                                                   preferred_element_type=F32)

    for d, ci in work:
        last = c - 1 if d == 0 else 0
        q_all = refs[d][0][rows(ci), :]
        outs = []
        for h in heads:
            st = st_ref[d, h]
            inter = lax.dot_general((q_all[:, hsl[h]] * blk(d, ci, 0, h)).astype(BF16),
                                    st.astype(BF16), _NT, preferred_element_type=F32)
            outs.append(intra[d, ci, h] + inter)
            decay = decay_x[d, ci][last:last + 1, hsl[h]]
            st_ref[d, h] = st * decay + increments[d, ci, h]
        refs[d][4][rows(ci), :] = jnp.concatenate(outs, axis=1).astype(refs[d][4].dtype)


def _gla_scan(q, k, v, lg, n_ctx):
    n_all, dk_total = q.shape
    dv_total = v.shape[1]
    t = ROW_TILE
    assert t % GLA_CHUNK == 0 and n_ctx == t
    n_tiles = n_all // t
    dmat, masks = _scan_tables()
    fwd = lambda i: (i, 0)
    bidx = lambda i: jnp.where(i == 0, 0, n_tiles - i)
    bwd = lambda i: (bidx(i), 0)
    bwd_g = lambda i: (bidx(i), 1)
    return pl.pallas_call(
        _gla_scan_kernel,
        grid=(n_tiles,),
        in_specs=[_const_spec(dmat.shape), _const_spec(masks.shape),
                  pl.BlockSpec((t, dk_total), fwd), pl.BlockSpec((t, dk_total), fwd),
                  pl.BlockSpec((t, dv_total), fwd), pl.BlockSpec((t, dk_total), fwd),
                  pl.BlockSpec((t, dk_total), bwd), pl.BlockSpec((t, dk_total), bwd),
                  pl.BlockSpec((t, dv_total), bwd), pl.BlockSpec((t, dk_total), bwd_g)],
        out_specs=[pl.BlockSpec((t, dv_total), fwd), pl.BlockSpec((t, dv_total), bwd)],
        out_shape=[jax.ShapeDtypeStruct((n_all, dv_total), BF16),
                   jax.ShapeDtypeStruct((n_all, dv_total), BF16)],
        scratch_shapes=[pltpu.VMEM((2, GLA_HEADS, dv_total // GLA_HEADS, dk_total // GLA_HEADS), F32)],
        compiler_params=_cparams("arbitrary"),
        name="gla_scan",
    )(jnp.asarray(dmat, BF16), jnp.asarray(masks, F32), q, k, v, lg, q, k, v, lg)


def _gla_readout_kernel(ctx_ref, x_ref, of_ref, ob_ref, r_ref, modv_ref, gn_ref, wo_ref, out_ref):
    d = x_ref.shape[1]
    dv = gn_ref.shape[1]
    is_ctx = pl.program_id(0) == 0
    xt = jnp.where(is_ctx, ctx_ref[...], x_ref[...])
    o = of_ref[...].astype(F32) + ob_ref[...].astype(F32)
    parts = []
    for h in range(GLA_HEADS):
        parts.append(_rms(o[:, h * dv:(h + 1) * dv]) * gn_ref[...])
    y = (jnp.concatenate(parts, axis=1) * r_ref[...].astype(F32)).astype(BF16)
    dx = jnp.dot(y, wo_ref[...], preferred_element_type=F32)
    out_ref[...] = xt + _mod_rows(modv_ref, is_ctx, 2, d) * dx


def _gla_readout(ctx2, x2, o_f, o_b, gr, modv, gn, wo):
    n_ctx, d = ctx2.shape
    n_all, dv_total = o_f.shape
    t = ROW_TILE
    row = lambda i: (i, 0)
    return pl.pallas_call(
        _gla_readout_kernel,
        grid=(n_all // t,),
        in_specs=[pl.BlockSpec((t, d), lambda i: (0, 0)),
                  pl.BlockSpec((t, d), lambda i: (jnp.maximum(i - 1, 0), 0)),
                  pl.BlockSpec((t, dv_total), row), pl.BlockSpec((t, dv_total), row),
                  pl.BlockSpec((t, dv_total), row),
                  _const_spec(modv.shape), _const_spec((1, gn.shape[0])), _const_spec(wo.shape)],
        out_specs=pl.BlockSpec((t, d), row),
        out_shape=jax.ShapeDtypeStruct((n_all, d), F32),
        compiler_params=_cparams("parallel"),
        name="gla_readout",
    )(ctx2, x2, o_f, o_b, gr, modv, gn.reshape(1, -1), wo.astype(BF16))


HALO = BF16_SUBLANES


def _ffn_kernel(prev_ref, x_ref, next_ref, modv_ref, g_ref, wup_ref, cw_ref, cb_ref, wdn_ref,
                fin_ref, out_ref, lhs_ref, u_ref, act_ref, *, ctx_tiles, final):
    t, d = x_ref.shape
    d_ff = wdn_ref.shape[0]
    i = pl.program_id(0)
    last = pl.num_programs(0) - 1
    is_ctx = i < ctx_tiles
    has_prev = jnp.logical_and(i != 0, i != ctx_tiles)
    has_next = jnp.logical_and(i != last, i != ctx_tiles - 1)
    g = g_ref[...]
    scale = _mod_rows(modv_ref, is_ctx, 4, d)
    shift = _mod_rows(modv_ref, is_ctx, 3, d)
    xt = x_ref[...]
    lhs_ref[0:HALO, :] = jnp.where(has_prev, _modnorm(prev_ref[...], g, scale, shift), 0.0).astype(BF16)
    lhs_ref[HALO:HALO + t, :] = _modnorm(xt, g, scale, shift).astype(BF16)
    lhs_ref[HALO + t:, :] = jnp.where(has_next, _modnorm(next_ref[...], g, scale, shift), 0.0).astype(BF16)
    lhs = lhs_ref[...]
    cc = FFN_COL_CHUNK
    n_chunks = d_ff // cc
    halves_of = lambda c: ((0, c * cc), (1, d_ff + c * cc))

    def up(c):
        for half, off in halves_of(c):
            u_ref[half, c] = jnp.dot(lhs, wup_ref[:, off:off + cc], preferred_element_type=F32)

    def gated(c):
        halves = []
        for half, off in halves_of(c):
            cols = slice(off, off + cc)
            u = u_ref[half, c]
            rows_all = u.shape[0]
            u_prev = pltpu.roll(u, 1, 0)[HALO:HALO + t, :]
            u_next = pltpu.roll(u, rows_all - 1, 0)[HALO:HALO + t, :]
            halves.append(u_prev * cw_ref[0:1, cols]
                          + u[HALO:HALO + t, :] * cw_ref[1:2, cols]
                          + u_next * cw_ref[2:3, cols]
                          + cb_ref[:, cols])
        return (_silu(halves[0]) * halves[1]).astype(BF16)

    for c in range(n_chunks):
        up(c)
        act_ref[:, c * cc:(c + 1) * cc] = gated(c)
    down = jnp.dot(act_ref[...], wdn_ref[...], preferred_element_type=F32)
    y = xt + _mod_rows(modv_ref, is_ctx, 5, d) * down
    if final:
        y = _rms(y) * fin_ref[...]
    out_ref[...] = y


def _conv_ffn(xs, modv, norm_g, w_up, w_conv, b_conv, w_down, final_g, *, ctx_tiles, final):
    n, d = xs.shape
    d_ff = w_down.shape[0]
    t = ROW_TILE
    assert d_ff % FFN_COL_CHUNK == 0
    per = t // HALO
    n_halo_blocks = n // HALO
    kern = functools.partial(_ffn_kernel, ctx_tiles=ctx_tiles, final=final)
    return pl.pallas_call(
        kern,
        grid=(n // t,),
        in_specs=[pl.BlockSpec((HALO, d), lambda i: (jnp.maximum(i * per - 1, 0), 0)),
                  pl.BlockSpec((t, d), lambda i: (i, 0)),
                  pl.BlockSpec((HALO, d), lambda i: (jnp.minimum((i + 1) * per, n_halo_blocks - 1), 0)),
                  _const_spec(modv.shape), _const_spec((1, d)),
                  _const_spec(w_up.shape), _const_spec(w_conv.shape), _const_spec((1, 2 * d_ff)),
                  _const_spec(w_down.shape), _const_spec((1, d))],
        out_specs=pl.BlockSpec((t, d), lambda i: (i, 0)),
        out_shape=jax.ShapeDtypeStruct((n, d), F32),
        scratch_shapes=[pltpu.VMEM((t + 2 * HALO, d), BF16),
                        pltpu.VMEM((2, d_ff // FFN_COL_CHUNK, t + 2 * HALO, FFN_COL_CHUNK), F32),
                        pltpu.VMEM((t, d_ff), BF16)],
        compiler_params=_cparams("parallel"),
        name="conv_ffn_final" if final else "conv_ffn",
    )(xs, xs, xs, modv, norm_g.reshape(1, d), w_up.astype(BF16), w_conv,
      b_conv.reshape(1, -1), w_down.astype(BF16), final_g.reshape(1, d))


def _rope_tables(n_tokens, width):
    rows = n_tokens // GRID_W
    row = jnp.repeat(jnp.arange(rows, dtype=F32), GRID_W)
    col = jnp.tile(jnp.arange(GRID_W, dtype=F32), rows)
    quarter = DIFF_HEAD_DIM // 4
    inv = ROPE_BASE ** (-jnp.arange(quarter, dtype=F32) / quarter)
    ang_r = row[:, None] * inv
    ang_c = col[:, None] * inv
    ang = jnp.concatenate([ang_r, ang_r, ang_c, ang_c], axis=-1)
    sign = jnp.tile(jnp.concatenate([-jnp.ones((quarter,), F32), jnp.ones((quarter,), F32)]), 2)
    reps = width // DIFF_HEAD_DIM
    return jnp.tile(jnp.cos(ang), (1, reps)), jnp.tile(jnp.sin(ang) * sign, (1, reps))


def _diff_proj_kernel(x_ref, cos_ref, sin_ref, modv_ref, g_ref, w_ref, qt_ref, k_ref, vt_ref):
    t, d = x_ref.shape
    is_ctx = pl.program_id(0) == 0
    h = _modnorm(x_ref[...], g_ref[...], _mod_rows(modv_ref, is_ctx, 1, d),
                 _mod_rows(modv_ref, is_ctx, 0, d)).astype(BF16)
    qkv = jnp.dot(h, w_ref[...], preferred_element_type=F32)
    n = qkv.shape[1] // 3
    quarter = DIFF_HEAD_DIM // 4
    reps = n // LANES
    cos = jnp.concatenate([cos_ref[...]] * reps, axis=1)
    sin = jnp.concatenate([sin_ref[...]] * reps, axis=1)
    lane = lax.broadcasted_iota(jnp.int32, (t, n), 1)
    first_half = (lane % (2 * quarter)) < quarter

    def rope(a):
        partner = jnp.where(first_half, pltpu.roll(a, n - quarter, 1), pltpu.roll(a, quarter, 1))
        return jnp.where(is_ctx, a, a * cos + partner * sin)

    q = rope(qkv[:, :n]) * (DIFF_HEAD_DIM ** -0.5 * math.log2(math.e))
    qt_ref[...] = q.T.astype(BF16)
    k = rope(qkv[:, n:2 * n])
    hw = 2 * DIFF_HEAD_DIM
    ones_col = jnp.where(lax.broadcasted_iota(jnp.int32, (t, ATT_KW - hw), 1) < SHIFT_PARTS, 1.0, 0.0)
    parts = []
    for hh in range(n // hw):
        parts += [k[:, hh * hw:(hh + 1) * hw], ones_col]
    k_ref[...] = jnp.concatenate(parts, axis=1).astype(BF16)
    vt_ref[...] = qkv[:, 2 * n:].T.astype(BF16)


def _diff_proj(xs, n_ctx, modv, norm_g, wq, wk, wv):
    n_all, d = xs.shape
    n_lat = n_all - n_ctx
    t = ROW_TILE
    n = wq.shape[1]
    w = jnp.concatenate([wq, wk, wv], axis=1).astype(BF16)
    cos, sin = _rope_tables(n_lat, LANES)
    kaug = n // (2 * DIFF_HEAD_DIM) * ATT_KW
    lat = lambda i: (jnp.maximum(i - 1, 0), 0)
    return pl.pallas_call(
        _diff_proj_kernel,
        grid=(n_all // t,),
        in_specs=[pl.BlockSpec((t, d), lambda i: (i, 0)),
                  pl.BlockSpec((t, LANES), lat), pl.BlockSpec((t, LANES), lat),
                  _const_spec(modv.shape), _const_spec((1, d)), _const_spec(w.shape)],
        out_specs=[pl.BlockSpec((n, t), lambda i: (0, jnp.maximum(i - 1, 0))),
                   pl.BlockSpec((t, kaug), lambda i: (i, 0)),
                   pl.BlockSpec((n, t), lambda i: (0, i))],
        out_shape=[jax.ShapeDtypeStruct((n, n_lat), BF16),
                   jax.ShapeDtypeStruct((n_all, kaug), BF16),
                   jax.ShapeDtypeStruct((n, n_all), BF16)],
        compiler_params=_cparams("arbitrary"),
        name="diff_proj",
    )(xs, cos, sin, modv, norm_g.reshape(1, d), w)


SHIFT_ROWS = BF16_SUBLANES
SHIFT_PARTS = 2


def _diff_attn_kernel(lam_ref, qt_ref, k_ref, vt_ref, o_ref, qa_ref, l_ref, acc_ref,
                      *, lambda_init):
    hd = DIFF_HEAD_DIM
    hw = 2 * hd
    tq = qt_ref.shape[1]
    kw = k_ref.shape[1]
    n_blocks = k_ref.shape[0] // ATT_TK
    tk = ATT_TK
    qt = qt_ref[...]
    rows = lax.broadcasted_iota(jnp.int32, qt.shape, 0)
    zero = jnp.zeros_like(qt)
    shift_row = lax.broadcasted_iota(jnp.int32, (SHIFT_ROWS, tq), 0)

    def block(j, br):
        start = j * tk if isinstance(j, int) else pl.multiple_of(j * tk, LANES)
        kb = k_ref[pl.ds(start, tk), :]
        vb = vt_ref[:, pl.ds(start, tk)]
        bm, cs, pv = [], [], []
        for sub in range(tq // ATT_SUB):
            cols = slice(sub * ATT_SUB, (sub + 1) * ATT_SUB)
            s = jnp.dot(kb, qa_ref[br, :, cols], preferred_element_type=F32)
            bm.append(jnp.max(s, axis=0, keepdims=True))
            p = jnp.exp2(s)
            cs.append(jnp.sum(p, axis=0, keepdims=True))
            pv.append(jnp.dot(vb, p.astype(BF16), preferred_element_type=F32))
        return (jnp.concatenate(bm, axis=1), jnp.concatenate(cs, axis=1),
                jnp.concatenate(pv, axis=1))

    for br in range(2):
        keep = jnp.logical_and(rows >= br * hd, rows < (br + 1) * hd)
        qa_ref[br, 0:hw, :] = jnp.where(keep, qt, zero)
        qa_ref[br, hw:, :] = jnp.zeros((kw - hw, tq), BF16)

    top = [None, None]
    for j in range(n_blocks):
        for br in range(2):
            bm, cs, pv = block(j, br)
            if j == 0:
                top[br] = bm
                l_ref[br] = cs
                acc_ref[br] = pv
            else:
                top[br] = jnp.maximum(top[br], bm)
                l_ref[br] += cs
                acc_ref[br] += pv

    @pl.when(jnp.max(jnp.maximum(jnp.abs(top[0]), jnp.abs(top[1]))) > ATT_OVERFLOW_LOG2)
    def _():
        for br in range(2):
            hi, lo = _split2(top[br])
            qa_ref[br, hw:hw + SHIFT_ROWS, :] = jnp.where(
                shift_row == 0, -hi.astype(F32),
                jnp.where(shift_row == 1, -lo.astype(F32), 0.0)).astype(BF16)
        l_ref[...] = jnp.zeros_like(l_ref)
        acc_ref[...] = jnp.zeros_like(acc_ref)

        def sum_body(j, carry):
            for br in range(2):
                _, cs, pv = block(j, br)
                l_ref[br] += cs
                acc_ref[br] += pv
            return carry

        lax.fori_loop(0, n_blocks, sum_body, 0)

    lam = (jnp.exp(jnp.sum(lam_ref[0:1, :] * lam_ref[1:2, :], axis=1, keepdims=True))
           - jnp.exp(jnp.sum(lam_ref[2:3, :] * lam_ref[3:4, :], axis=1, keepdims=True))
           + lambda_init)
    o_ref[...] = (acc_ref[0] / l_ref[0] - lam * (acc_ref[1] / l_ref[1])).astype(o_ref.dtype)


def _diff_attn(qt, k, vt, lam_vecs, lambda_init):
    n, n_lat = qt.shape
    n_all = k.shape[0]
    hw = 2 * DIFF_HEAD_DIM
    heads = n // hw
    tq = ATT_TQ
    assert n_lat % tq == 0 and n_all % ATT_TK == 0 and tq % ATT_SUB == 0
    kern = functools.partial(_diff_attn_kernel, lambda_init=lambda_init)
    return pl.pallas_call(
        kern,
        grid=(heads, n_lat // tq),
        in_specs=[pl.BlockSpec(lam_vecs.shape, lambda h, i: (0, 0)),
                  pl.BlockSpec((hw, tq), lambda h, i: (h, i)),
                  pl.BlockSpec((n_all, ATT_KW), lambda h, i: (0, h)),
                  pl.BlockSpec((hw, n_all), lambda h, i: (h, 0))],
        out_specs=pl.BlockSpec((hw, tq), lambda h, i: (h, i)),
        out_shape=jax.ShapeDtypeStruct((n, n_lat), BF16),
        scratch_shapes=[pltpu.VMEM((2, ATT_KW, tq), BF16), pltpu.VMEM((2, 1, tq), F32),
                        pltpu.VMEM((2, hw, tq), F32)],
        compiler_params=_cparams("arbitrary", "arbitrary"),
        name="diff_attn",
    )(lam_vecs, qt, k, vt)


def _diff_readout_kernel(x_ref, ot_ref, modv_ref, sub_ref, wo_ref, out_ref, *, out_scale):
    d = x_ref.shape[1]
    hw = sub_ref.shape[0]
    ot = ot_ref[...].astype(F32)
    parts = []
    for h in range(ot.shape[0] // hw):
        oh = ot[h * hw:(h + 1) * hw, :]
        ms = jnp.mean(oh * oh, axis=0, keepdims=True)
        parts.append(oh * lax.rsqrt(ms + EPS) * sub_ref[...] * out_scale)
    y = jnp.concatenate(parts, axis=0).T.astype(BF16)
    dx = jnp.dot(y, wo_ref[...], preferred_element_type=F32)
    out_ref[...] = x_ref[...] + modv_ref[0:1, 2 * d:3 * d] * dx


def _diff_readout(xs, n_ctx, ot, modv, subln, wo, lambda_init):
    n, n_lat = ot.shape
    d = xs.shape[1]
    t = ROW_TILE
    ctx_tiles = n_ctx // t
    kern = functools.partial(_diff_readout_kernel, out_scale=1.0 - lambda_init)
    return pl.pallas_call(
        kern,
        grid=(n_lat // t,),
        in_specs=[pl.BlockSpec((t, d), lambda i: (i + ctx_tiles, 0)),
                  pl.BlockSpec((n, t), lambda i: (0, i)),
                  _const_spec(modv.shape), _const_spec((subln.shape[0], 1)), _const_spec(wo.shape)],
        out_specs=pl.BlockSpec((t, d), lambda i: (i, 0)),
        out_shape=jax.ShapeDtypeStruct((n_lat, d), F32),
        compiler_params=_cparams("parallel"),
        name="diff_readout",
    )(xs, ot, modv, subln.reshape(-1, 1), wo.astype(BF16))


def kernel(x, c, ctx, c_ctx, mod_w, mod_b, norm_mix, norm_ffn, gla_wq, gla_wk, gla_wv, gla_wr, gla_wg1, gla_wg2, gla_bg, gla_norm, gla_wo, diff_wq, diff_wk, diff_wv, diff_lq1, diff_lk1, diff_lq2, diff_lk2, diff_subln, diff_wo, ffn_wup, ffn_conv, ffn_conv_b, ffn_wdown, final_norm):
    bsz, n_lat, d = x.shape
    n_ctx = ctx.shape[1]
    depth = mod_w.shape[0]
    assert bsz == 1 and depth == 2, "layer 0 is the GLA mixer, layer 1 the differential attention"
    x2 = x.reshape(n_lat, d)
    ctx2 = ctx.reshape(n_ctx, d)
    cc = jnp.zeros((8, d), F32).at[0].set(c[0]).at[1].set(c_ctx)
    mod = _modulation(cc, mod_w, mod_b)

    q, k, v, gr, lg = _gla_proj(ctx2, x2, mod[0], norm_mix[0], gla_wq[0], gla_wk[0], gla_wv[0],
                                gla_wr[0], gla_wg1[0], gla_wg2[0], gla_bg[0])
    o_f, o_b = _gla_scan(q, k, v, lg, n_ctx)
    xs = _gla_readout(ctx2, x2, o_f, o_b, gr, mod[0], gla_norm[0], gla_wo[0])
    xs = _conv_ffn(xs, mod[0], norm_ffn[0], ffn_wup[0], ffn_conv[0], ffn_conv_b[0], ffn_wdown[0],
                   final_norm, ctx_tiles=n_ctx // ROW_TILE, final=False)

    lambda_init = 0.8 - 0.6 * math.exp(-0.3 * 1)
    qt, kk, vt = _diff_proj(xs, n_ctx, mod[1], norm_mix[1], diff_wq[0], diff_wk[0], diff_wv[0])
    lam_vecs = jnp.zeros((8, DIFF_HEAD_DIM), F32).at[0:4].set(
        jnp.concatenate([diff_lq1, diff_lk1, diff_lq2, diff_lk2], axis=0))
    ot = _diff_attn(qt, kk, vt, lam_vecs, lambda_init)
    xl = _diff_readout(xs, n_ctx, ot, mod[1], diff_subln[0], diff_wo[0], lambda_init)
    out = _conv_ffn(xl, mod[1], norm_ffn[1], ffn_wup[1], ffn_conv[1], ffn_conv_b[1], ffn_wdown[1],
                    final_norm, ctx_tiles=0, final=True)
    return out.reshape(bsz, n_lat, d)
```

```python
import functools
import math

import numpy as np
import jax
import jax.numpy as jnp
from jax import lax
from jax.experimental import pallas as pl
from jax.experimental.pallas import tpu as pltpu

GRID_W = 64
GLA_HEADS = 4
GLA_GATE_RANK = 16
GLA_GATE_NORM = 16.0
GLA_CHUNK = 64
DIFF_HEAD_DIM = 64
ROPE_BASE = 10000.0
CONV_WIDTH = 3
N_MOD = 6
EPS = 1e-6

LANES = 128
BF16_SUBLANES = 16
VMEM_LIMIT = 56 * 1024 * 1024

ROW_TILE = 256
FFN_COL_CHUNK = 256
ATT_TQ = 512
ATT_SUB = 512
ATT_TK = 3328
ATT_KW = 256
ATT_OVERFLOW_LOG2 = 64.0
MOD_COL_TILE = 1536

F32 = jnp.float32
BF16 = jnp.bfloat16

_NT = (((1,), (1,)), ((), ()))
_TN = (((0,), (0,)), ((), ()))


def _cparams(*sem):
    return pltpu.CompilerParams(dimension_semantics=sem, vmem_limit_bytes=VMEM_LIMIT)


def _const_spec(shape):
    nd = len(shape)
    return pl.BlockSpec(shape, lambda *_: (0,) * nd, pipeline_mode=pl.Buffered(1))


def _silu(x):
    return x * (1.0 / (1.0 + jnp.exp(-x)))


def _rms(x):
    return x * lax.rsqrt(jnp.mean(x * x, axis=-1, keepdims=True) + EPS)


def _mod_rows(modv_ref, is_ctx, k, d):
    sl = slice(k * d, (k + 1) * d)
    return jnp.where(is_ctx, modv_ref[1:2, sl], modv_ref[0:1, sl])


def _modnorm(x, g, scale, shift):
    return (_rms(x) * g) * (1.0 + scale) + shift


def _split2(x):
    hi = x.astype(BF16)
    lo = (x - hi.astype(F32)).astype(BF16)
    return hi, lo


def _mod_kernel(cc_ref, w_ref, b_ref, o_ref):
    s = _silu(cc_ref[...])
    w = w_ref[0]
    s_hi, s_lo = _split2(s)
    w_hi, w_lo = _split2(w)
    acc = jnp.dot(s_hi, w_hi, preferred_element_type=F32)
    acc += jnp.dot(s_lo, w_hi, preferred_element_type=F32)
    acc += jnp.dot(s_hi, w_lo, preferred_element_type=F32)
    o_ref[0] = acc + b_ref[0]


def _modulation(cc, mod_w, mod_b):
    depth, d, n = mod_w.shape
    tn = MOD_COL_TILE
    return pl.pallas_call(
        _mod_kernel,
        grid=(depth, n // tn),
        in_specs=[pl.BlockSpec((8, d), lambda l, j: (0, 0)),
                  pl.BlockSpec((1, d, tn), lambda l, j: (l, 0, j)),
                  pl.BlockSpec((1, 1, tn), lambda l, j: (l, 0, j))],
        out_specs=pl.BlockSpec((1, 8, tn), lambda l, j: (l, 0, j)),
        out_shape=jax.ShapeDtypeStruct((depth, 8, n), F32),
        compiler_params=_cparams("parallel", "parallel"),
        name="modulation",
    )(cc, mod_w, mod_b.reshape(depth, 1, n))


def _gla_proj_kernel(ctx_ref, x_ref, modv_ref, g_ref, w_ref, wg1_ref, wg2_ref, bg_ref,
                     q_ref, k_ref, v_ref, r_ref, lg_ref, *, dk_total, dv_total, q_scale):
    d = x_ref.shape[1]
    is_ctx = pl.program_id(0) == 0
    xt = jnp.where(is_ctx, ctx_ref[...], x_ref[...])
    h = _modnorm(xt, g_ref[...], _mod_rows(modv_ref, is_ctx, 1, d),
                 _mod_rows(modv_ref, is_ctx, 0, d)).astype(BF16)
    qkvr = jnp.dot(h, w_ref[...], preferred_element_type=F32)
    o1 = dk_total
    o2 = 2 * dk_total
    o3 = o2 + dv_total
    q_ref[...] = qkvr[:, :o1] * q_scale
    k_ref[...] = qkvr[:, o1:o2]
    v_ref[...] = qkvr[:, o2:o3].astype(BF16)
    r_ref[...] = _silu(qkvr[:, o3:]).astype(BF16)
    low = jnp.dot(h, wg1_ref[...], preferred_element_type=F32).astype(BF16)
    z = jnp.dot(low, wg2_ref[...], preferred_element_type=F32) + bg_ref[...]
    log_sig = jnp.minimum(z, 0.0) - jnp.log1p(jnp.exp(-jnp.abs(z)))
    lg_ref[...] = log_sig * (1.0 / GLA_GATE_NORM)


def _gla_proj(ctx2, x2, modv, norm_g, wq, wk, wv, wr, wg1, wg2, bg):
    n_ctx, d = ctx2.shape
    n_lat = x2.shape[0]
    t = ROW_TILE
    assert n_ctx == t and n_lat % t == 0
    n_all = n_ctx + n_lat
    dk_total = wq.shape[1]
    dv_total = wv.shape[1]
    rank = wg1.shape[2]
    w = jnp.concatenate([wq, wk, wv, wr], axis=1).astype(BF16)
    wg1c = jnp.zeros((d, LANES), F32).at[:, :rank].set(wg1[0]).at[:, rank:2 * rank].set(wg1[1])
    wg2c = (jnp.zeros((LANES, 2 * dk_total), F32)
            .at[:rank, :dk_total].set(wg2[0]).at[rank:2 * rank, dk_total:].set(wg2[1]))
    bgc = bg.reshape(1, 2 * dk_total)
    row = lambda i: (i, 0)
    kern = functools.partial(_gla_proj_kernel, dk_total=dk_total, dv_total=dv_total,
                             q_scale=(dk_total // GLA_HEADS) ** -0.5)
    return pl.pallas_call(
        kern,
        grid=(n_all // t,),
        in_specs=[pl.BlockSpec((t, d), lambda i: (0, 0)),
                  pl.BlockSpec((t, d), lambda i: (jnp.maximum(i - 1, 0), 0)),
                  _const_spec(modv.shape), _const_spec((1, d)),
                  _const_spec(w.shape), _const_spec(wg1c.shape), _const_spec(wg2c.shape),
                  _const_spec(bgc.shape)],
        out_specs=[pl.BlockSpec((t, dk_total), row), pl.BlockSpec((t, dk_total), row),
                   pl.BlockSpec((t, dv_total), row), pl.BlockSpec((t, dv_total), row),
                   pl.BlockSpec((t, 2 * dk_total), row)],
        out_shape=[jax.ShapeDtypeStruct((n_all, dk_total), F32),
                   jax.ShapeDtypeStruct((n_all, dk_total), F32),
                   jax.ShapeDtypeStruct((n_all, dv_total), BF16),
                   jax.ShapeDtypeStruct((n_all, dv_total), BF16),
                   jax.ShapeDtypeStruct((n_all, 2 * dk_total), F32)],
        compiler_params=_cparams("parallel"),
        name="gla_proj",
    )(ctx2, x2, modv, norm_g.reshape(1, d), w, wg1c.astype(BF16), wg2c.astype(BF16), bgc)


N_LEVELS = int(math.log2(GLA_CHUNK))
N_DECAY_BLOCKS = 2 + N_LEVELS


def _scan_tables():
    c = GLA_CHUNK
    t = np.arange(c)[:, None]
    u = np.arange(c)[None, :]
    blocks = [(u <= t), (u > t)]
    masks = [(t == u)]
    for lvl in range(1, N_LEVELS + 1):
        size = 1 << lvl
        half = size // 2
        start = (t // size) * size
        mid = start + half - 1
        upper = (t - start) >= half
        blocks.append((upper & (u > mid) & (u <= t)) | ((~upper) & (u > t) & (u <= mid)))
        same = (t // size) == (u // size)
        masks.append(same & upper & ((u - (u // size) * size) < half))
    fwd = np.concatenate(blocks, axis=0).astype(np.float32)
    fmask = np.concatenate(masks, axis=0).astype(np.float32)
    rev = lambda m: m.reshape(-1, c, c)[:, ::-1, ::-1].reshape(-1, c)
    twice = lambda m: np.concatenate([m, m], axis=1)
    return (np.stack([twice(fwd), twice(rev(fwd))]), np.stack([fmask, rev(fmask)]))


def _gla_scan_kernel(dmat_ref, mask_ref, qf, kf, vf, gf, qb, kb, vb, gb, of_ref, ob_ref, st_ref):
    @pl.when(pl.program_id(0) == 0)
    def _():
        st_ref[...] = jnp.zeros_like(st_ref)

    c = GLA_CHUNK
    n_sub = qf.shape[0] // c
    dk = qf.shape[1] // GLA_HEADS
    dv = vf.shape[1] // GLA_HEADS
    heads = range(GLA_HEADS)
    hsl = [slice(h * dk, (h + 1) * dk) for h in heads]
    vsl = [slice(h * dv, (h + 1) * dv) for h in heads]
    refs = ((qf, kf, vf, gf, of_ref), (qb, kb, vb, gb, ob_ref))
    work = [(d, ci) for step in range(n_sub) for d, ci in ((0, step), (1, n_sub - 1 - step))]
    rows = lambda ci: slice(ci * c, (ci + 1) * c)

    decay_x = {}
    for d, ci in work:
        g_hi, g_lo = _split2(refs[d][3][rows(ci), :])
        e = jnp.dot(dmat_ref[d], jnp.concatenate([g_hi, g_lo], axis=0),
                    preferred_element_type=F32)
        decay_x[d, ci] = jnp.exp(e)
    blk = lambda d, ci, n, h: decay_x[d, ci][n * c:(n + 1) * c, hsl[h]]

    scores = {}
    for d, ci in work:
        q_all = refs[d][0][rows(ci), :]
        k_all = refs[d][1][rows(ci), :]
        for h in heads:
            q = q_all[:, hsl[h]]
            k = k_all[:, hsl[h]]
            parts = [lax.dot_general(q.astype(BF16), k.astype(BF16), _NT,
                                     preferred_element_type=F32)]
            for lvl in range(1, N_LEVELS + 1):
                f = blk(d, ci, 1 + lvl, h)
                parts.append(lax.dot_general((q * f).astype(BF16), (k * f).astype(BF16), _NT,
                                             preferred_element_type=F32))
            scores[d, ci, h] = parts

    increments = {}
    intra = {}
    for d, ci in work:
        k_all = refs[d][1][rows(ci), :]
        v_all = refs[d][2][rows(ci), :]
        for h in heads:
            a = mask_ref[d, 0:c, :] * scores[d, ci, h][0]
            for lvl in range(1, N_LEVELS + 1):
                a += mask_ref[d, lvl * c:(lvl + 1) * c, :] * scores[d, ci, h][lvl]
            intra[d, ci, h] = jnp.dot(a.astype(BF16), v_all[:, vsl[h]], preferred_element_type=F32)
            k_end = (k_all[:, hsl[h]] * blk(d, ci, 1, h)).astype(BF16)
            increments[d, ci, h] = lax.dot_general(v_all[:, vsl[h]], k_end, _TN,
                                                   preferred_element_type=F32)

    for d, ci in work:
        last = c - 1 if d == 0 else 0
        q_all = refs[d][0][rows(ci), :]
        outs = []
        for h in heads:
            st = st_ref[d, h]
            inter = lax.dot_general((q_all[:, hsl[h]] * blk(d, ci, 0, h)).astype(BF16),
                                    st.astype(BF16), _NT, preferred_element_type=F32)
            outs.append(intra[d, ci, h] + inter)
            decay = decay_x[d, ci][last:last + 1, hsl[h]]
            st_ref[d, h] = st * decay + increments[d, ci, h]
        refs[d][4][rows(ci), :] = jnp.concatenate(outs, axis=1).astype(refs[d][4].dtype)


def _gla_scan(q, k, v, lg, n_ctx):
    n_all, dk_total = q.shape
    dv_total = v.shape[1]
    t = ROW_TILE
    assert t % GLA_CHUNK == 0 and n_ctx == t
    n_tiles = n_all // t
    dmat, masks = _scan_tables()
    fwd = lambda i: (i, 0)
    bidx = lambda i: jnp.where(i == 0, 0, n_tiles - i)
    bwd = lambda i: (bidx(i), 0)
    bwd_g = lambda i: (bidx(i), 1)
    return pl.pallas_call(
        _gla_scan_kernel,
        grid=(n_tiles,),
        in_specs=[_const_spec(dmat.shape), _const_spec(masks.shape),
                  pl.BlockSpec((t, dk_total), fwd), pl.BlockSpec((t, dk_total), fwd),
                  pl.BlockSpec((t, dv_total), fwd), pl.BlockSpec((t, dk_total), fwd),
                  pl.BlockSpec((t, dk_total), bwd), pl.BlockSpec((t, dk_total), bwd),
                  pl.BlockSpec((t, dv_total), bwd), pl.BlockSpec((t, dk_total), bwd_g)],
        out_specs=[pl.BlockSpec((t, dv_total), fwd), pl.BlockSpec((t, dv_total), bwd)],
        out_shape=[jax.ShapeDtypeStruct((n_all, dv_total), BF16),
                   jax.ShapeDtypeStruct((n_all, dv_total), BF16)],
        scratch_shapes=[pltpu.VMEM((2, GLA_HEADS, dv_total // GLA_HEADS, dk_total // GLA_HEADS), F32)],
        compiler_params=_cparams("arbitrary"),
        name="gla_scan",
    )(jnp.asarray(dmat, BF16), jnp.asarray(masks, F32), q, k, v, lg, q, k, v, lg)


def _gla_readout_kernel(ctx_ref, x_ref, of_ref, ob_ref, r_ref, modv_ref, gn_ref, wo_ref, out_ref):
    d = x_ref.shape[1]
    dv = gn_ref.shape[1]
    is_ctx = pl.program_id(0) == 0
    xt = jnp.where(is_ctx, ctx_ref[...], x_ref[...])
    o = of_ref[...].astype(F32) + ob_ref[...].astype(F32)
    parts = []
    for h in range(GLA_HEADS):
        parts.append(_rms(o[:, h * dv:(h + 1) * dv]) * gn_ref[...])
    y = (jnp.concatenate(parts, axis=1) * r_ref[...].astype(F32)).astype(BF16)
    dx = jnp.dot(y, wo_ref[...], preferred_element_type=F32)
    out_ref[...] = xt + _mod_rows(modv_ref, is_ctx, 2, d) * dx


def _gla_readout(ctx2, x2, o_f, o_b, gr, modv, gn, wo):
    n_ctx, d = ctx2.shape
    n_all, dv_total = o_f.shape
    t = ROW_TILE
    row = lambda i: (i, 0)
    return pl.pallas_call(
        _gla_readout_kernel,
        grid=(n_all // t,),
        in_specs=[pl.BlockSpec((t, d), lambda i: (0, 0)),
                  pl.BlockSpec((t, d), lambda i: (jnp.maximum(i - 1, 0), 0)),
                  pl.BlockSpec((t, dv_total), row), pl.BlockSpec((t, dv_total), row),
                  pl.BlockSpec((t, dv_total), row),
                  _const_spec(modv.shape), _const_spec((1, gn.shape[0])), _const_spec(wo.shape)],
        out_specs=pl.BlockSpec((t, d), row),
        out_shape=jax.ShapeDtypeStruct((n_all, d), F32),
        compiler_params=_cparams("parallel"),
        name="gla_readout",
    )(ctx2, x2, o_f, o_b, gr, modv, gn.reshape(1, -1), wo.astype(BF16))


HALO = BF16_SUBLANES


def _ffn_subtile(sub, n_sub, n_tiles, ctx_tiles, final, prev_ref, x_ref, next_ref, modv_ref, g_ref,
                 wup_ref, cw_ref, cb_ref, wdn_ref, fin_ref, out_ref, lhs_ref, u_ref, act_ref):
    t = ROW_TILE
    d = x_ref.shape[1]
    d_ff = wdn_ref.shape[0]
    s = pl.program_id(0) * n_sub + sub
    is_ctx = s < ctx_tiles
    has_prev = jnp.logical_and(s != 0, s != ctx_tiles)
    has_next = jnp.logical_and(s != n_tiles - 1, s != ctx_tiles - 1)
    r0 = sub * t
    prev = prev_ref[...] if sub == 0 else x_ref[r0 - HALO:r0, :]
    nxt = next_ref[...] if sub == n_sub - 1 else x_ref[r0 + t:r0 + t + HALO, :]
    g = g_ref[...]
    scale = _mod_rows(modv_ref, is_ctx, 4, d)
    shift = _mod_rows(modv_ref, is_ctx, 3, d)
    xt = x_ref[r0:r0 + t, :]
    lhs_ref[sub, 0:HALO, :] = jnp.where(has_prev, _modnorm(prev, g, scale, shift), 0.0).astype(BF16)
    lhs_ref[sub, HALO:HALO + t, :] = _modnorm(xt, g, scale, shift).astype(BF16)
    lhs_ref[sub, HALO + t:, :] = jnp.where(has_next, _modnorm(nxt, g, scale, shift), 0.0).astype(BF16)
    yield
    lhs = lhs_ref[sub]
    cc = FFN_COL_CHUNK
    for c in range(d_ff // cc):
        halves = []
        for half, off in ((0, c * cc), (1, d_ff + c * cc)):
            cols = slice(off, off + cc)
            u_ref[sub, half, c] = jnp.dot(lhs, wup_ref[:, cols], preferred_element_type=F32)
            u = u_ref[sub, half, c]
            u_prev = pltpu.roll(u, 1, 0)[HALO:HALO + t, :]
            u_next = pltpu.roll(u, u.shape[0] - 1, 0)[HALO:HALO + t, :]
            halves.append(u_prev * cw_ref[0:1, cols]
                          + u[HALO:HALO + t, :] * cw_ref[1:2, cols]
                          + u_next * cw_ref[2:3, cols]
                          + cb_ref[:, cols])
        act_ref[sub, :, c * cc:(c + 1) * cc] = (_silu(halves[0]) * halves[1]).astype(BF16)
    yield
    down = jnp.dot(act_ref[sub], wdn_ref[...], preferred_element_type=F32)
    y = xt + _mod_rows(modv_ref, is_ctx, 5, d) * down
    if final:
        y = _rms(y) * fin_ref[...]
    out_ref[r0:r0 + t, :] = y


def _ffn_kernel(*refs, n_sub, n_tiles, ctx_tiles, final):
    tiles = [_ffn_subtile(sub, n_sub, n_tiles, ctx_tiles, final, *refs) for sub in range(n_sub)]
    for _phase in range(3):
        for tile in tiles:
            next(tile, None)


def _conv_ffn(xs, modv, norm_g, w_up, w_conv, b_conv, w_down, final_g, *, ctx_tiles, final):
    n, d = xs.shape
    d_ff = w_down.shape[0]
    t = ROW_TILE
    assert d_ff % FFN_COL_CHUNK == 0
    n_tiles = n // t
    n_sub = 2 if n_tiles % 2 == 0 else 1
    per = n_sub * t // HALO
    n_halo_blocks = n // HALO
    kern = functools.partial(_ffn_kernel, n_sub=n_sub, n_tiles=n_tiles, ctx_tiles=ctx_tiles,
                             final=final)
    return pl.pallas_call(
        kern,
        grid=(n_tiles // n_sub,),
        in_specs=[pl.BlockSpec((HALO, d), lambda i: (jnp.maximum(i * per - 1, 0), 0)),
                  pl.BlockSpec((n_sub * t, d), lambda i: (i, 0)),
                  pl.BlockSpec((HALO, d), lambda i: (jnp.minimum((i + 1) * per, n_halo_blocks - 1), 0)),
                  _const_spec(modv.shape), _const_spec((1, d)),
                  _const_spec(w_up.shape), _const_spec(w_conv.shape), _const_spec((1, 2 * d_ff)),
                  _const_spec(w_down.shape), _const_spec((1, d))],
        out_specs=pl.BlockSpec((n_sub * t, d), lambda i: (i, 0)),
        out_shape=jax.ShapeDtypeStruct((n, d), F32),
        scratch_shapes=[pltpu.VMEM((n_sub, t + 2 * HALO, d), BF16),
                        pltpu.VMEM((n_sub, 2, d_ff // FFN_COL_CHUNK, t + 2 * HALO, FFN_COL_CHUNK), F32),
                        pltpu.VMEM((n_sub, t, d_ff), BF16)],
        compiler_params=_cparams("parallel"),
        name="conv_ffn_final" if final else "conv_ffn",
    )(xs, xs, xs, modv, norm_g.reshape(1, d), w_up.astype(BF16), w_conv,
      b_conv.reshape(1, -1), w_down.astype(BF16), final_g.reshape(1, d))


def _rope_tables(n_tokens, width):
    rows = n_tokens // GRID_W
    row = jnp.repeat(jnp.arange(rows, dtype=F32), GRID_W)
    col = jnp.tile(jnp.arange(GRID_W, dtype=F32), rows)
    quarter = DIFF_HEAD_DIM // 4
    inv = ROPE_BASE ** (-jnp.arange(quarter, dtype=F32) / quarter)
    ang_r = row[:, None] * inv
    ang_c = col[:, None] * inv
    ang = jnp.concatenate([ang_r, ang_r, ang_c, ang_c], axis=-1)
    sign = jnp.tile(jnp.concatenate([-jnp.ones((quarter,), F32), jnp.ones((quarter,), F32)]), 2)
    reps = width // DIFF_HEAD_DIM
    return jnp.tile(jnp.cos(ang), (1, reps)), jnp.tile(jnp.sin(ang) * sign, (1, reps))


def _diff_proj_kernel(x_ref, cos_ref, sin_ref, modv_ref, g_ref, w_ref, qt_ref, k_ref, vt_ref):
    t, d = x_ref.shape
    is_ctx = pl.program_id(0) == 0
    h = _modnorm(x_ref[...], g_ref[...], _mod_rows(modv_ref, is_ctx, 1, d),
                 _mod_rows(modv_ref, is_ctx, 0, d)).astype(BF16)
    n = w_ref.shape[1] // 3
    quarter = DIFF_HEAD_DIM // 4
    reps = n // LANES
    cos = jnp.concatenate([cos_ref[...]] * reps, axis=1)
    sin = jnp.concatenate([sin_ref[...]] * reps, axis=1)
    lane = lax.broadcasted_iota(jnp.int32, (t, n), 1)
    first_half = (lane % (2 * quarter)) < quarter

    def rope(a):
        partner = jnp.where(first_half, pltpu.roll(a, n - quarter, 1), pltpu.roll(a, quarter, 1))
        return jnp.where(is_ctx, a, a * cos + partner * sin)

    qkv = jnp.dot(h, w_ref[...], preferred_element_type=F32)
    q = rope(qkv[:, :n]) * (DIFF_HEAD_DIM ** -0.5 * math.log2(math.e))
    qt_ref[...] = q.T.astype(BF16)
    k = rope(qkv[:, n:2 * n])
    hw = 2 * DIFF_HEAD_DIM
    ones_col = jnp.where(lax.broadcasted_iota(jnp.int32, (t, ATT_KW - hw), 1) < SHIFT_PARTS, 1.0, 0.0)
    parts = []
    for hh in range(n // hw):
        parts += [k[:, hh * hw:(hh + 1) * hw], ones_col]
    k_ref[...] = jnp.concatenate(parts, axis=1).astype(BF16)
    vt_ref[...] = qkv[:, 2 * n:].T.astype(BF16)


def _diff_proj(xs, n_ctx, modv, norm_g, wq, wk, wv):
    n_all, d = xs.shape
    n_lat = n_all - n_ctx
    t = ROW_TILE
    n = wq.shape[1]
    w = jnp.concatenate([wq, wk, wv], axis=1).astype(BF16)
    cos, sin = _rope_tables(n_lat, LANES)
    kaug = n // (2 * DIFF_HEAD_DIM) * ATT_KW
    lat = lambda i: (jnp.maximum(i - 1, 0), 0)
    return pl.pallas_call(
        _diff_proj_kernel,
        grid=(n_all // t,),
        in_specs=[pl.BlockSpec((t, d), lambda i: (i, 0)),
                  pl.BlockSpec((t, LANES), lat), pl.BlockSpec((t, LANES), lat),
                  _const_spec(modv.shape), _const_spec((1, d)), _const_spec(w.shape)],
        out_specs=[pl.BlockSpec((n, t), lambda i: (0, jnp.maximum(i - 1, 0))),
                   pl.BlockSpec((t, kaug), lambda i: (i, 0)),
                   pl.BlockSpec((n, t), lambda i: (0, i))],
        out_shape=[jax.ShapeDtypeStruct((n, n_lat), BF16),
                   jax.ShapeDtypeStruct((n_all, kaug), BF16),
                   jax.ShapeDtypeStruct((n, n_all), BF16)],
        compiler_params=_cparams("arbitrary"),
        name="diff_proj",
    )(xs, cos, sin, modv, norm_g.reshape(1, d), w)


SHIFT_ROWS = BF16_SUBLANES
SHIFT_PARTS = 2


def _diff_attn_kernel(lam_ref, qt_ref, k_ref, vt_ref, o_ref, qa_ref, l_ref, acc_ref,
                      *, lambda_init):
    hd = DIFF_HEAD_DIM
    hw = 2 * hd
    tq = qt_ref.shape[1]
    kw = k_ref.shape[1]
    n_blocks = k_ref.shape[0] // ATT_TK
    tk = ATT_TK
    qt = qt_ref[...]
    rows = lax.broadcasted_iota(jnp.int32, qt.shape, 0)
    zero = jnp.zeros_like(qt)
    shift_row = lax.broadcasted_iota(jnp.int32, (SHIFT_ROWS, tq), 0)

    def block(j, br):
        start = j * tk if isinstance(j, int) else pl.multiple_of(j * tk, LANES)
        kb = k_ref[pl.ds(start, tk), :]
        vb = vt_ref[:, pl.ds(start, tk)]
        bm, cs, pv = [], [], []
        for sub in range(tq // ATT_SUB):
            cols = slice(sub * ATT_SUB, (sub + 1) * ATT_SUB)
            s = jnp.dot(kb, qa_ref[br, :, cols], preferred_element_type=F32)
            bm.append(jnp.max(s, axis=0, keepdims=True))
            p = jnp.exp2(s)
            cs.append(jnp.sum(p, axis=0, keepdims=True))
            pv.append(jnp.dot(vb, p.astype(BF16), preferred_element_type=F32))
        return (jnp.concatenate(bm, axis=1), jnp.concatenate(cs, axis=1),
                jnp.concatenate(pv, axis=1))

    for br in range(2):
        keep = jnp.logical_and(rows >= br * hd, rows < (br + 1) * hd)
        qa_ref[br, 0:hw, :] = jnp.where(keep, qt, zero)
        qa_ref[br, hw:, :] = jnp.zeros((kw - hw, tq), BF16)

    top = [None, None]
    for j in range(n_blocks):
        for br in range(2):
            bm, cs, pv = block(j, br)
            if j == 0:
                top[br] = bm
                l_ref[br] = cs
                acc_ref[br] = pv
            else:
                top[br] = jnp.maximum(top[br], bm)
                l_ref[br] += cs
                acc_ref[br] += pv

    @pl.when(jnp.max(jnp.maximum(jnp.abs(top[0]), jnp.abs(top[1]))) > ATT_OVERFLOW_LOG2)
    def _():
        for br in range(2):
            hi, lo = _split2(top[br])
            qa_ref[br, hw:hw + SHIFT_ROWS, :] = jnp.where(
                shift_row == 0, -hi.astype(F32),
                jnp.where(shift_row == 1, -lo.astype(F32), 0.0)).astype(BF16)
        l_ref[...] = jnp.zeros_like(l_ref)
        acc_ref[...] = jnp.zeros_like(acc_ref)

        def sum_body(j, carry):
            for br in range(2):
                _, cs, pv = block(j, br)
                l_ref[br] += cs
                acc_ref[br] += pv
            return carry

        lax.fori_loop(0, n_blocks, sum_body, 0)

    lam = (jnp.exp(jnp.sum(lam_ref[0:1, :] * lam_ref[1:2, :], axis=1, keepdims=True))
           - jnp.exp(jnp.sum(lam_ref[2:3, :] * lam_ref[3:4, :], axis=1, keepdims=True))
           + lambda_init)
    o_ref[...] = (acc_ref[0] / l_ref[0] - lam * (acc_ref[1] / l_ref[1])).astype(o_ref.dtype)


def _diff_attn(qt, k, vt, lam_vecs, lambda_init):
    n, n_lat = qt.shape
    n_all = k.shape[0]
    hw = 2 * DIFF_HEAD_DIM
    heads = n // hw
    tq = ATT_TQ
    assert n_lat % tq == 0 and n_all % ATT_TK == 0 and tq % ATT_SUB == 0
    kern = functools.partial(_diff_attn_kernel, lambda_init=lambda_init)
    return pl.pallas_call(
        kern,
        grid=(heads, n_lat // tq),
        in_specs=[pl.BlockSpec(lam_vecs.shape, lambda h, i: (0, 0)),
                  pl.BlockSpec((hw, tq), lambda h, i: (h, i)),
                  pl.BlockSpec((n_all, ATT_KW), lambda h, i: (0, h)),
                  pl.BlockSpec((hw, n_all), lambda h, i: (h, 0))],
        out_specs=pl.BlockSpec((hw, tq), lambda h, i: (h, i)),
        out_shape=jax.ShapeDtypeStruct((n, n_lat), BF16),
        scratch_shapes=[pltpu.VMEM((2, ATT_KW, tq), BF16), pltpu.VMEM((2, 1, tq), F32),
                        pltpu.VMEM((2, hw, tq), F32)],
        compiler_params=_cparams("arbitrary", "arbitrary"),
        name="diff_attn",
    )(lam_vecs, qt, k, vt)


def _diff_readout_kernel(x_ref, ot_ref, modv_ref, sub_ref, wo_ref, out_ref, *, out_scale):
    d = x_ref.shape[1]
    hw = sub_ref.shape[0]
    ot = ot_ref[...].astype(F32)
    parts = []
    for h in range(ot.shape[0] // hw):
        oh = ot[h * hw:(h + 1) * hw, :]
        ms = jnp.mean(oh * oh, axis=0, keepdims=True)
        parts.append(oh * lax.rsqrt(ms + EPS) * sub_ref[...] * out_scale)
    y = jnp.concatenate(parts, axis=0).T.astype(BF16)
    dx = jnp.dot(y, wo_ref[...], preferred_element_type=F32)
    out_ref[...] = x_ref[...] + modv_ref[0:1, 2 * d:3 * d] * dx


def _diff_readout(xs, n_ctx, ot, modv, subln, wo, lambda_init):
    n, n_lat = ot.shape
    d = xs.shape[1]
    t = ROW_TILE
    ctx_tiles = n_ctx // t
    kern = functools.partial(_diff_readout_kernel, out_scale=1.0 - lambda_init)
    return pl.pallas_call(
        kern,
        grid=(n_lat // t,),
        in_specs=[pl.BlockSpec((t, d), lambda i: (i + ctx_tiles, 0)),
                  pl.BlockSpec((n, t), lambda i: (0, i)),
                  _const_spec(modv.shape), _const_spec((subln.shape[0], 1)), _const_spec(wo.shape)],
        out_specs=pl.BlockSpec((t, d), lambda i: (i, 0)),
        out_shape=jax.ShapeDtypeStruct((n_lat, d), F32),
        compiler_params=_cparams("parallel"),
        name="diff_readout",
    )(xs, ot, modv, subln.reshape(-1, 1), wo.astype(BF16))


def kernel(x, c, ctx, c_ctx, mod_w, mod_b, norm_mix, norm_ffn, gla_wq, gla_wk, gla_wv, gla_wr, gla_wg1, gla_wg2, gla_bg, gla_norm, gla_wo, diff_wq, diff_wk, diff_wv, diff_lq1, diff_lk1, diff_lq2, diff_lk2, diff_subln, diff_wo, ffn_wup, ffn_conv, ffn_conv_b, ffn_wdown, final_norm):
    bsz, n_lat, d = x.shape
    n_ctx = ctx.shape[1]
    depth = mod_w.shape[0]
    assert bsz == 1 and depth == 2, "layer 0 is the GLA mixer, layer 1 the differential attention"
    x2 = x.reshape(n_lat, d)
    ctx2 = ctx.reshape(n_ctx, d)
    cc = jnp.zeros((8, d), F32).at[0].set(c[0]).at[1].set(c_ctx)
    mod = _modulation(cc, mod_w, mod_b)

    q, k, v, gr, lg = _gla_proj(ctx2, x2, mod[0], norm_mix[0], gla_wq[0], gla_wk[0], gla_wv[0],
                                gla_wr[0], gla_wg1[0], gla_wg2[0], gla_bg[0])
    o_f, o_b = _gla_scan(q, k, v, lg, n_ctx)
    xs = _gla_readout(ctx2, x2, o_f, o_b, gr, mod[0], gla_norm[0], gla_wo[0])
    xs = _conv_ffn(xs, mod[0], norm_ffn[0], ffn_wup[0], ffn_conv[0], ffn_conv_b[0], ffn_wdown[0],
                   final_norm, ctx_tiles=n_ctx // ROW_TILE, final=False)

    lambda_init = 0.8 - 0.6 * math.exp(-0.3 * 1)
    qt, kk, vt = _diff_proj(xs, n_ctx, mod[1], norm_mix[1], diff_wq[0], diff_wk[0], diff_wv[0])
    lam_vecs = jnp.zeros((8, DIFF_HEAD_DIM), F32).at[0:4].set(
        jnp.concatenate([diff_lq1, diff_lk1, diff_lq2, diff_lk2], axis=0))
    ot = _diff_attn(qt, kk, vt, lam_vecs, lambda_init)
    xl = _diff_readout(xs, n_ctx, ot, mod[1], diff_subln[0], diff_wo[0], lambda_init)
    out = _conv_ffn(xl, mod[1], norm_ffn[1], ffn_wup[1], ffn_conv[1], ffn_conv_b[1], ffn_wdown[1],
                    final_norm, ctx_tiles=0, final=True)
    return out.reshape(bsz, n_lat, d)
```

```python
import functools
import math

import numpy as np
import jax
import jax.numpy as jnp
from jax import lax
from jax.experimental import pallas as pl
from jax.experimental.pallas import tpu as pltpu

GRID_W = 64
GLA_HEADS = 4
GLA_GATE_RANK = 16
GLA_GATE_NORM = 16.0
GLA_CHUNK = 64
DIFF_HEAD_DIM = 64
ROPE_BASE = 10000.0
CONV_WIDTH = 3
N_MOD = 6
EPS = 1e-6

LANES = 128
BF16_SUBLANES = 16
VMEM_LIMIT = 56 * 1024 * 1024

ROW_TILE = 256
FFN_COL_CHUNK = 256
ATT_TQ = 512
ATT_SUB = 512
ATT_TK = 3328
ATT_KW = 256
ATT_OVERFLOW_LOG2 = 64.0
MOD_COL_TILE = 1536

F32 = jnp.float32
BF16 = jnp.bfloat16

_NT = (((1,), (1,)), ((), ()))
_TN = (((0,), (0,)), ((), ()))


def _cparams(*sem):
    return pltpu.CompilerParams(dimension_semantics=sem, vmem_limit_bytes=VMEM_LIMIT)


def _const_spec(shape):
    nd = len(shape)
    return pl.BlockSpec(shape, lambda *_: (0,) * nd, pipeline_mode=pl.Buffered(1))


def _silu(x):
    return x * (1.0 / (1.0 + jnp.exp(-x)))


def _rms(x):
    return x * lax.rsqrt(jnp.mean(x * x, axis=-1, keepdims=True) + EPS)


def _mod_rows(modv_ref, is_ctx, k, d):
    sl = slice(k * d, (k + 1) * d)
    return jnp.where(is_ctx, modv_ref[1:2, sl], modv_ref[0:1, sl])


def _modnorm(x, g, scale, shift):
    return (_rms(x) * g) * (1.0 + scale) + shift


def _split2(x):
    hi = x.astype(BF16)
    lo = (x - hi.astype(F32)).astype(BF16)
    return hi, lo


def _mod_kernel(cc_ref, w_ref, b_ref, o_ref):
    s = _silu(cc_ref[...])
    w = w_ref[0]
    s_hi, s_lo = _split2(s)
    w_hi, w_lo = _split2(w)
    acc = jnp.dot(s_hi, w_hi, preferred_element_type=F32)
    acc += jnp.dot(s_lo, w_hi, preferred_element_type=F32)
    acc += jnp.dot(s_hi, w_lo, preferred_element_type=F32)
    o_ref[0] = acc + b_ref[0]


def _modulation(cc, mod_w, mod_b):
    depth, d, n = mod_w.shape
    tn = MOD_COL_TILE
    return pl.pallas_call(
        _mod_kernel,
        grid=(depth, n // tn),
        in_specs=[pl.BlockSpec((8, d), lambda l, j: (0, 0)),
                  pl.BlockSpec((1, d, tn), lambda l, j: (l, 0, j)),
                  pl.BlockSpec((1, 1, tn), lambda l, j: (l, 0, j))],
        out_specs=pl.BlockSpec((1, 8, tn), lambda l, j: (l, 0, j)),
        out_shape=jax.ShapeDtypeStruct((depth, 8, n), F32),
        compiler_params=_cparams("parallel", "parallel"),
        name="modulation",
    )(cc, mod_w, mod_b.reshape(depth, 1, n))


def _gla_proj_kernel(ctx_ref, x_ref, modv_ref, g_ref, w_ref, wg1_ref, wg2_ref, bg_ref,
                     q_ref, k_ref, v_ref, r_ref, lg_ref, *, dk_total, dv_total, q_scale):
    d = x_ref.shape[1]
    is_ctx = pl.program_id(0) == 0
    xt = jnp.where(is_ctx, ctx_ref[...], x_ref[...])
    h = _modnorm(xt, g_ref[...], _mod_rows(modv_ref, is_ctx, 1, d),
                 _mod_rows(modv_ref, is_ctx, 0, d)).astype(BF16)
    qkvr = jnp.dot(h, w_ref[...], preferred_element_type=F32)
    o1 = dk_total
    o2 = 2 * dk_total
    o3 = o2 + dv_total
    q_ref[...] = qkvr[:, :o1] * q_scale
    k_ref[...] = qkvr[:, o1:o2]
    v_ref[...] = qkvr[:, o2:o3].astype(BF16)
    r_ref[...] = _silu(qkvr[:, o3:]).astype(BF16)
    low = jnp.dot(h, wg1_ref[...], preferred_element_type=F32).astype(BF16)
    z = jnp.dot(low, wg2_ref[...], preferred_element_type=F32) + bg_ref[...]
    log_sig = jnp.minimum(z, 0.0) - jnp.log1p(jnp.exp(-jnp.abs(z)))
    lg_ref[...] = log_sig * (1.0 / GLA_GATE_NORM)


def _gla_proj(ctx2, x2, modv, norm_g, wq, wk, wv, wr, wg1, wg2, bg):
    n_ctx, d = ctx2.shape
    n_lat = x2.shape[0]
    t = ROW_TILE
    assert n_ctx == t and n_lat % t == 0
    n_all = n_ctx + n_lat
    dk_total = wq.shape[1]
    dv_total = wv.shape[1]
    rank = wg1.shape[2]
    w = jnp.concatenate([wq, wk, wv, wr], axis=1).astype(BF16)
    wg1c = jnp.zeros((d, LANES), F32).at[:, :rank].set(wg1[0]).at[:, rank:2 * rank].set(wg1[1])
    wg2c = (jnp.zeros((LANES, 2 * dk_total), F32)
            .at[:rank, :dk_total].set(wg2[0]).at[rank:2 * rank, dk_total:].set(wg2[1]))
    bgc = bg.reshape(1, 2 * dk_total)
    row = lambda i: (i, 0)
    kern = functools.partial(_gla_proj_kernel, dk_total=dk_total, dv_total=dv_total,
                             q_scale=(dk_total // GLA_HEADS) ** -0.5)
    return pl.pallas_call(
        kern,
        grid=(n_all // t,),
        in_specs=[pl.BlockSpec((t, d), lambda i: (0, 0)),
                  pl.BlockSpec((t, d), lambda i: (jnp.maximum(i - 1, 0), 0)),
                  _const_spec(modv.shape), _const_spec((1, d)),
                  _const_spec(w.shape), _const_spec(wg1c.shape), _const_spec(wg2c.shape),
                  _const_spec(bgc.shape)],
        out_specs=[pl.BlockSpec((t, dk_total), row), pl.BlockSpec((t, dk_total), row),
                   pl.BlockSpec((t, dv_total), row), pl.BlockSpec((t, dv_total), row),
                   pl.BlockSpec((t, 2 * dk_total), row)],
        out_shape=[jax.ShapeDtypeStruct((n_all, dk_total), F32),
                   jax.ShapeDtypeStruct((n_all, dk_total), F32),
                   jax.ShapeDtypeStruct((n_all, dv_total), BF16),
                   jax.ShapeDtypeStruct((n_all, dv_total), BF16),
                   jax.ShapeDtypeStruct((n_all, 2 * dk_total), F32)],
        compiler_params=_cparams("parallel"),
        name="gla_proj",
    )(ctx2, x2, modv, norm_g.reshape(1, d), w, wg1c.astype(BF16), wg2c.astype(BF16), bgc)


N_LEVELS = int(math.log2(GLA_CHUNK))
N_DECAY_BLOCKS = 2 + N_LEVELS


def _scan_tables():
    c = GLA_CHUNK
    t = np.arange(c)[:, None]
    u = np.arange(c)[None, :]
    blocks = [(u <= t), (u > t)]
    masks = [(t == u)]
    for lvl in range(1, N_LEVELS + 1):
        size = 1 << lvl
        half = size // 2
        start = (t // size) * size
        mid = start + half - 1
        upper = (t - start) >= half
        blocks.append((upper & (u > mid) & (u <= t)) | ((~upper) & (u > t) & (u <= mid)))
        same = (t // size) == (u // size)
        masks.append(same & upper & ((u - (u // size) * size) < half))
    fwd = np.concatenate(blocks, axis=0).astype(np.float32)
    fmask = np.concatenate(masks, axis=0).astype(np.float32)
    rev = lambda m: m.reshape(-1, c, c)[:, ::-1, ::-1].reshape(-1, c)
    twice = lambda m: np.concatenate([m, m], axis=1)
    return (np.stack([twice(fwd), twice(rev(fwd))]), np.stack([fmask, rev(fmask)]))


def _gla_scan_kernel(dmat_ref, mask_ref, qf, kf, vf, gf, qb, kb, vb, gb, of_ref, ob_ref, st_ref):
    @pl.when(pl.program_id(0) == 0)
    def _():
        st_ref[...] = jnp.zeros_like(st_ref)

    c = GLA_CHUNK
    n_sub = qf.shape[0] // c
    dk = qf.shape[1] // GLA_HEADS
    dv = vf.shape[1] // GLA_HEADS
    heads = range(GLA_HEADS)
    hsl = [slice(h * dk, (h + 1) * dk) for h in heads]
    vsl = [slice(h * dv, (h + 1) * dv) for h in heads]
    refs = ((qf, kf, vf, gf, of_ref), (qb, kb, vb, gb, ob_ref))
    work = [(d, ci) for step in range(n_sub) for d, ci in ((0, step), (1, n_sub - 1 - step))]
    rows = lambda ci: slice(ci * c, (ci + 1) * c)

    decay_x = {}
    for d, ci in work:
        g_hi, g_lo = _split2(refs[d][3][rows(ci), :])
        e = jnp.dot(dmat_ref[d], jnp.concatenate([g_hi, g_lo], axis=0),
                    preferred_element_type=F32)
        decay_x[d, ci] = jnp.exp(e)
    blk = lambda d, ci, n, h: decay_x[d, ci][n * c:(n + 1) * c, hsl[h]]

    scores = {}
    for d, ci in work:
        q_all = refs[d][0][rows(ci), :]
        k_all = refs[d][1][rows(ci), :]
        for h in heads:
            q = q_all[:, hsl[h]]
            k = k_all[:, hsl[h]]
            parts = [lax.dot_general(q.astype(BF16), k.astype(BF16), _NT,
                                     preferred_element_type=F32)]
            for lvl in range(1, N_LEVELS + 1):
                f = blk(d, ci, 1 + lvl, h)
                parts.append(lax.dot_general((q * f).astype(BF16), (k * f).astype(BF16), _NT,
                                             preferred_element_type=F32))
            scores[d, ci, h] = parts

    increments = {}
    intra = {}
    for d, ci in work:
        k_all = refs[d][1][rows(ci), :]
        v_all = refs[d][2][rows(ci), :]
        for h in heads:
            a = mask_ref[d, 0:c, :] * scores[d, ci, h][0]
            for lvl in range(1, N_LEVELS + 1):
                a += mask_ref[d, lvl * c:(lvl + 1) * c, :] * scores[d, ci, h][lvl]
            intra[d, ci, h] = jnp.dot(a.astype(BF16), v_all[:, vsl[h]], preferred_element_type=F32)
            k_end = (k_all[:, hsl[h]] * blk(d, ci, 1, h)).astype(BF16)
            increments[d, ci, h] = lax.dot_general(v_all[:, vsl[h]], k_end, _TN,
                                                   preferred_element_type=F32)

    for d, ci in work:
        last = c - 1 if d == 0 else 0
        q_all = refs[d][0][rows(ci), :]
        outs = []
        for h in heads:
            st = st_ref[d, h]
            inter = lax.dot_general((q_all[:, hsl[h]] * blk(d, ci, 0, h)).astype(BF16),
                                    st.astype(BF16), _NT, preferred_element_type=F32)
            outs.append(intra[d, ci, h] + inter)
            decay = decay_x[d, ci][last:last + 1, hsl[h]]
            st_ref[d, h] = st * decay + increments[d, ci, h]
        refs[d][4][rows(ci), :] = jnp.concatenate(outs, axis=1).astype(refs[d][4].dtype)


def _gla_scan(q, k, v, lg, n_ctx):
    n_all, dk_total = q.shape
    dv_total = v.shape[1]
    t = ROW_TILE
    assert t % GLA_CHUNK == 0 and n_ctx == t
    n_tiles = n_all // t
    dmat, masks = _scan_tables()
    fwd = lambda i: (i, 0)
    bidx = lambda i: jnp.where(i == 0, 0, n_tiles - i)
    bwd = lambda i: (bidx(i), 0)
    bwd_g = lambda i: (bidx(i), 1)
    return pl.pallas_call(
        _gla_scan_kernel,
        grid=(n_tiles,),
        in_specs=[_const_spec(dmat.shape), _const_spec(masks.shape),
                  pl.BlockSpec((t, dk_total), fwd), pl.BlockSpec((t, dk_total), fwd),
                  pl.BlockSpec((t, dv_total), fwd), pl.BlockSpec((t, dk_total), fwd),
                  pl.BlockSpec((t, dk_total), bwd), pl.BlockSpec((t, dk_total), bwd),
                  pl.BlockSpec((t, dv_total), bwd), pl.BlockSpec((t, dk_total), bwd_g)],
        out_specs=[pl.BlockSpec((t, dv_total), fwd), pl.BlockSpec((t, dv_total), bwd)],
        out_shape=[jax.ShapeDtypeStruct((n_all, dv_total), BF16),
                   jax.ShapeDtypeStruct((n_all, dv_total), BF16)],
        scratch_shapes=[pltpu.VMEM((2, GLA_HEADS, dv_total // GLA_HEADS, dk_total // GLA_HEADS), F32)],
        compiler_params=_cparams("arbitrary"),
        name="gla_scan",
    )(jnp.asarray(dmat, BF16), jnp.asarray(masks, F32), q, k, v, lg, q, k, v, lg)


def _gla_readout_kernel(ctx_ref, x_ref, of_ref, ob_ref, r_ref, modv_ref, gn_ref, wo_ref, out_ref):
    d = x_ref.shape[1]
    dv = gn_ref.shape[1]
    is_ctx = pl.program_id(0) == 0
    xt = jnp.where(is_ctx, ctx_ref[...], x_ref[...])
    o = of_ref[...].astype(F32) + ob_ref[...].astype(F32)
    parts = []
    for h in range(GLA_HEADS):
        parts.append(_rms(o[:, h * dv:(h + 1) * dv]) * gn_ref[...])
    y = (jnp.concatenate(parts, axis=1) * r_ref[...].astype(F32)).astype(BF16)
    dx = jnp.dot(y, wo_ref[...], preferred_element_type=F32)
    out_ref[...] = xt + _mod_rows(modv_ref, is_ctx, 2, d) * dx


def _gla_readout(ctx2, x2, o_f, o_b, gr, modv, gn, wo):
    n_ctx, d = ctx2.shape
    n_all, dv_total = o_f.shape
    t = ROW_TILE
    row = lambda i: (i, 0)
    return pl.pallas_call(
        _gla_readout_kernel,
        grid=(n_all // t,),
        in_specs=[pl.BlockSpec((t, d), lambda i: (0, 0)),
                  pl.BlockSpec((t, d), lambda i: (jnp.maximum(i - 1, 0), 0)),
                  pl.BlockSpec((t, dv_total), row), pl.BlockSpec((t, dv_total), row),
                  pl.BlockSpec((t, dv_total), row),
                  _const_spec(modv.shape), _const_spec((1, gn.shape[0])), _const_spec(wo.shape)],
        out_specs=pl.BlockSpec((t, d), row),
        out_shape=jax.ShapeDtypeStruct((n_all, d), F32),
        compiler_params=_cparams("parallel"),
        name="gla_readout",
    )(ctx2, x2, o_f, o_b, gr, modv, gn.reshape(1, -1), wo.astype(BF16))


HALO = BF16_SUBLANES


def _ffn_subtile(sub, n_sub, n_tiles, ctx_tiles, final, prev_ref, x_ref, next_ref, modv_ref, g_ref,
                 wup_ref, cw_ref, cb_ref, wdn_ref, fin_ref, out_ref, lhs_ref, u_ref, act_ref):
    t = ROW_TILE
    d = x_ref.shape[1]
    d_ff = wdn_ref.shape[0]
    s = pl.program_id(0) * n_sub + sub
    is_ctx = s < ctx_tiles
    has_prev = jnp.logical_and(s != 0, s != ctx_tiles)
    has_next = jnp.logical_and(s != n_tiles - 1, s != ctx_tiles - 1)
    r0 = sub * t
    prev = prev_ref[...] if sub == 0 else x_ref[r0 - HALO:r0, :]
    nxt = next_ref[...] if sub == n_sub - 1 else x_ref[r0 + t:r0 + t + HALO, :]
    g = g_ref[...]
    scale = _mod_rows(modv_ref, is_ctx, 4, d)
    shift = _mod_rows(modv_ref, is_ctx, 3, d)
    xt = x_ref[r0:r0 + t, :]
    lhs_ref[sub, 0:HALO, :] = jnp.where(has_prev, _modnorm(prev, g, scale, shift), 0.0).astype(BF16)
    lhs_ref[sub, HALO:HALO + t, :] = _modnorm(xt, g, scale, shift).astype(BF16)
    lhs_ref[sub, HALO + t:, :] = jnp.where(has_next, _modnorm(nxt, g, scale, shift), 0.0).astype(BF16)
    yield
    lhs = lhs_ref[sub]
    cc = FFN_COL_CHUNK
    for c in range(d_ff // cc):
        halves = []
        for half, off in ((0, c * cc), (1, d_ff + c * cc)):
            cols = slice(off, off + cc)
            u_ref[sub, half, c] = jnp.dot(lhs, wup_ref[:, cols], preferred_element_type=F32)
            u = u_ref[sub, half, c]
            u_prev = pltpu.roll(u, 1, 0)[HALO:HALO + t, :]
            u_next = pltpu.roll(u, u.shape[0] - 1, 0)[HALO:HALO + t, :]
            halves.append(u_prev * cw_ref[0:1, cols]
                          + u[HALO:HALO + t, :] * cw_ref[1:2, cols]
                          + u_next * cw_ref[2:3, cols]
                          + cb_ref[:, cols])
        act_ref[sub, :, c * cc:(c + 1) * cc] = (_silu(halves[0]) * halves[1]).astype(BF16)
    yield
    down = jnp.dot(act_ref[sub], wdn_ref[...], preferred_element_type=F32)
    y = xt + _mod_rows(modv_ref, is_ctx, 5, d) * down
    if final:
        y = _rms(y) * fin_ref[...]
    out_ref[r0:r0 + t, :] = y


def _ffn_kernel(*refs, n_sub, n_tiles, ctx_tiles, final):
    tiles = [_ffn_subtile(sub, n_sub, n_tiles, ctx_tiles, final, *refs) for sub in range(n_sub)]
    for _phase in range(3):
        for tile in tiles:
            next(tile, None)


def _conv_ffn(xs, modv, norm_g, w_up, w_conv, b_conv, w_down, final_g, *, ctx_tiles, final):
    n, d = xs.shape
    d_ff = w_down.shape[0]
    t = ROW_TILE
    assert d_ff % FFN_COL_CHUNK == 0
    n_tiles = n // t
    n_sub = 2 if n_tiles % 2 == 0 else 1
    per = n_sub * t // HALO
    n_halo_blocks = n // HALO
    kern = functools.partial(_ffn_kernel, n_sub=n_sub, n_tiles=n_tiles, ctx_tiles=ctx_tiles,
                             final=final)
    return pl.pallas_call(
        kern,
        grid=(n_tiles // n_sub,),
        in_specs=[pl.BlockSpec((HALO, d), lambda i: (jnp.maximum(i * per - 1, 0), 0)),
                  pl.BlockSpec((n_sub * t, d), lambda i: (i, 0)),
                  pl.BlockSpec((HALO, d), lambda i: (jnp.minimum((i + 1) * per, n_halo_blocks - 1), 0)),
                  _const_spec(modv.shape), _const_spec((1, d)),
                  _const_spec(w_up.shape), _const_spec(w_conv.shape), _const_spec((1, 2 * d_ff)),
                  _const_spec(w_down.shape), _const_spec((1, d))],
        out_specs=pl.BlockSpec((n_sub * t, d), lambda i: (i, 0)),
        out_shape=jax.ShapeDtypeStruct((n, d), F32),
        scratch_shapes=[pltpu.VMEM((n_sub, t + 2 * HALO, d), BF16),
                        pltpu.VMEM((n_sub, 2, d_ff // FFN_COL_CHUNK, t + 2 * HALO, FFN_COL_CHUNK), F32),
                        pltpu.VMEM((n_sub, t, d_ff), BF16)],
        compiler_params=_cparams("parallel"),
        name="conv_ffn_final" if final else "conv_ffn",
    )(xs, xs, xs, modv, norm_g.reshape(1, d), w_up.astype(BF16), w_conv,
      b_conv.reshape(1, -1), w_down.astype(BF16), final_g.reshape(1, d))


def _rope_tables(n_tokens, width):
    rows = n_tokens // GRID_W
    row = jnp.repeat(jnp.arange(rows, dtype=F32), GRID_W)
    col = jnp.tile(jnp.arange(GRID_W, dtype=F32), rows)
    quarter = DIFF_HEAD_DIM // 4
    inv = ROPE_BASE ** (-jnp.arange(quarter, dtype=F32) / quarter)
    ang_r = row[:, None] * inv
    ang_c = col[:, None] * inv
    ang = jnp.concatenate([ang_r, ang_r, ang_c, ang_c], axis=-1)
    sign = jnp.tile(jnp.concatenate([-jnp.ones((quarter,), F32), jnp.ones((quarter,), F32)]), 2)
    reps = width // DIFF_HEAD_DIM
    return jnp.tile(jnp.cos(ang), (1, reps)), jnp.tile(jnp.sin(ang) * sign, (1, reps))


def _diff_proj_kernel(x_ref, cos_ref, sin_ref, modv_ref, g_ref, w_ref, qt_ref, k_ref, vt_ref):
    t, d = x_ref.shape
    is_ctx = pl.program_id(0) == 0
    h = _modnorm(x_ref[...], g_ref[...], _mod_rows(modv_ref, is_ctx, 1, d),
                 _mod_rows(modv_ref, is_ctx, 0, d)).astype(BF16)
    n = w_ref.shape[1] // 3
    quarter = DIFF_HEAD_DIM // 4
    reps = n // LANES
    cos = jnp.concatenate([cos_ref[...]] * reps, axis=1)
    sin = jnp.concatenate([sin_ref[...]] * reps, axis=1)
    lane = lax.broadcasted_iota(jnp.int32, (t, n), 1)
    first_half = (lane % (2 * quarter)) < quarter

    def rope(a):
        partner = jnp.where(first_half, pltpu.roll(a, n - quarter, 1), pltpu.roll(a, quarter, 1))
        return jnp.where(is_ctx, a, a * cos + partner * sin)

    qkv = jnp.dot(h, w_ref[...], preferred_element_type=F32)
    q = rope(qkv[:, :n]) * (DIFF_HEAD_DIM ** -0.5 * math.log2(math.e))
    qt_ref[...] = q.T.astype(BF16)
    k = rope(qkv[:, n:2 * n])
    hw = 2 * DIFF_HEAD_DIM
    ones_col = jnp.where(lax.broadcasted_iota(jnp.int32, (t, ATT_KW - hw), 1) < SHIFT_PARTS, 1.0, 0.0)
    parts = []
    for hh in range(n // hw):
        parts += [k[:, hh * hw:(hh + 1) * hw], ones_col]
    k_ref[...] = jnp.concatenate(parts, axis=1).astype(BF16)
    vt_ref[...] = qkv[:, 2 * n:].T.astype(BF16)


def _diff_proj(xs, n_ctx, modv, norm_g, wq, wk, wv):
    n_all, d = xs.shape
    n_lat = n_all - n_ctx
    t = ROW_TILE
    n = wq.shape[1]
    w = jnp.concatenate([wq, wk, wv], axis=1).astype(BF16)
    cos, sin = _rope_tables(n_lat, LANES)
    kaug = n // (2 * DIFF_HEAD_DIM) * ATT_KW
    lat = lambda i: (jnp.maximum(i - 1, 0), 0)
    return pl.pallas_call(
        _diff_proj_kernel,
        grid=(n_all // t,),
        in_specs=[pl.BlockSpec((t, d), lambda i: (i, 0)),
                  pl.BlockSpec((t, LANES), lat), pl.BlockSpec((t, LANES), lat),
                  _const_spec(modv.shape), _const_spec((1, d)), _const_spec(w.shape)],
        out_specs=[pl.BlockSpec((n, t), lambda i: (0, jnp.maximum(i - 1, 0))),
                   pl.BlockSpec((t, kaug), lambda i: (i, 0)),
                   pl.BlockSpec((n, t), lambda i: (0, i))],
        out_shape=[jax.ShapeDtypeStruct((n, n_lat), BF16),
                   jax.ShapeDtypeStruct((n_all, kaug), BF16),
                   jax.ShapeDtypeStruct((n, n_all), BF16)],
        compiler_params=_cparams("arbitrary"),
        name="diff_proj",
    )(xs, cos, sin, modv, norm_g.reshape(1, d), w)


SHIFT_ROWS = BF16_SUBLANES
SHIFT_PARTS = 2


def _diff_attn_kernel(lam_ref, qt_ref, k_ref, vt_ref, o_ref, qa_ref, l_ref, acc_ref,
                      *, lambda_init):
    hd = DIFF_HEAD_DIM
    hw = 2 * hd
    tq = qt_ref.shape[1]
    kw = k_ref.shape[1]
    n_blocks = k_ref.shape[0] // ATT_TK
    tk = ATT_TK
    qt = qt_ref[...]
    rows = lax.broadcasted_iota(jnp.int32, qt.shape, 0)
    zero = jnp.zeros_like(qt)
    shift_row = lax.broadcasted_iota(jnp.int32, (SHIFT_ROWS, tq), 0)

    def block(j, br):
        start = j * tk if isinstance(j, int) else pl.multiple_of(j * tk, LANES)
        kb = k_ref[pl.ds(start, tk), :]
        vb = vt_ref[:, pl.ds(start, tk)]
        bm, cs, pv = [], [], []
        for sub in range(tq // ATT_SUB):
            cols = slice(sub * ATT_SUB, (sub + 1) * ATT_SUB)
            s = jnp.dot(kb, qa_ref[br, :, cols], preferred_element_type=F32)
            bm.append(jnp.max(s, axis=0, keepdims=True))
            p = jnp.exp2(s)
            cs.append(jnp.sum(p, axis=0, keepdims=True))
            pv.append(jnp.dot(vb, p.astype(BF16), preferred_element_type=F32))
        return (jnp.concatenate(bm, axis=1), jnp.concatenate(cs, axis=1),
                jnp.concatenate(pv, axis=1))

    for br in range(2):
        keep = jnp.logical_and(rows >= br * hd, rows < (br + 1) * hd)
        qa_ref[br, 0:hw, :] = jnp.where(keep, qt, zero)
        qa_ref[br, hw:, :] = jnp.zeros((kw - hw, tq), BF16)

    top = [None, None]
    for j in range(n_blocks):
        for br in range(2):
            bm, cs, pv = block(j, br)
            if j == 0:
                top[br] = bm
                l_ref[br] = cs
                acc_ref[br] = pv
            else:
                top[br] = jnp.maximum(top[br], bm)
                l_ref[br] += cs
                acc_ref[br] += pv

    @pl.when(jnp.max(jnp.maximum(jnp.abs(top[0]), jnp.abs(top[1]))) > ATT_OVERFLOW_LOG2)
    def _():
        for br in range(2):
            hi, lo = _split2(top[br])
            qa_ref[br, hw:hw + SHIFT_ROWS, :] = jnp.where(
                shift_row == 0, -hi.astype(F32),
                jnp.where(shift_row == 1, -lo.astype(F32), 0.0)).astype(BF16)
        l_ref[...] = jnp.zeros_like(l_ref)
        acc_ref[...] = jnp.zeros_like(acc_ref)

        def sum_body(j, carry):
            for br in range(2):
                _, cs, pv = block(j, br)
                l_ref[br] += cs
                acc_ref[br] += pv
            return carry

        lax.fori_loop(0, n_blocks, sum_body, 0)

    lam = (jnp.exp(jnp.sum(lam_ref[0:1, :] * lam_ref[1:2, :], axis=1, keepdims=True))
           - jnp.exp(jnp.sum(lam_ref[2:3, :] * lam_ref[3:4, :], axis=1, keepdims=True))
           + lambda_init)
    o_ref[...] = (acc_ref[0] / l_ref[0] - lam * (acc_ref[1] / l_ref[1])).astype(o_ref.dtype)


def _diff_attn(qt, k, vt, lam_vecs, lambda_init):
    n, n_lat = qt.shape
    n_all = k.shape[0]
    hw = 2 * DIFF_HEAD_DIM
    heads = n // hw
    tq = ATT_TQ
    assert n_lat % tq == 0 and n_all % ATT_TK == 0 and tq % ATT_SUB == 0
    kern = functools.partial(_diff_attn_kernel, lambda_init=lambda_init)
    return pl.pallas_call(
        kern,
        grid=(heads, n_lat // tq),
        in_specs=[pl.BlockSpec(lam_vecs.shape, lambda h, i: (0, 0)),
                  pl.BlockSpec((hw, tq), lambda h, i: (h, i)),
                  pl.BlockSpec((n_all, ATT_KW), lambda h, i: (0, h)),
                  pl.BlockSpec((hw, n_all), lambda h, i: (h, 0))],
        out_specs=pl.BlockSpec((hw, tq), lambda h, i: (h, i)),
        out_shape=jax.ShapeDtypeStruct((n, n_lat), BF16),
        scratch_shapes=[pltpu.VMEM((2, ATT_KW, tq), BF16), pltpu.VMEM((2, 1, tq), F32),
                        pltpu.VMEM((2, hw, tq), F32)],
        compiler_params=_cparams("arbitrary", "arbitrary"),
        name="diff_attn",
    )(lam_vecs, qt, k, vt)


READOUT_TILES = 2


def _diff_readout_kernel(*refs, out_scale):
    x_refs = refs[:READOUT_TILES]
    ot_ref, modv_ref, sub_ref, wo_ref, out_ref = refs[READOUT_TILES:]
    t, d = x_refs[0].shape
    hw = sub_ref.shape[0]
    ot = ot_ref[...].astype(F32)
    parts = []
    for h in range(ot.shape[0] // hw):
        oh = ot[h * hw:(h + 1) * hw, :]
        ms = jnp.mean(oh * oh, axis=0, keepdims=True)
        parts.append(oh * lax.rsqrt(ms + EPS) * sub_ref[...] * out_scale)
    y = jnp.concatenate(parts, axis=0).T.astype(BF16)
    dx = jnp.dot(y, wo_ref[...], preferred_element_type=F32)
    gate = modv_ref[0:1, 2 * d:3 * d]
    for j, x_ref in enumerate(x_refs):
        out_ref[j * t:(j + 1) * t, :] = x_ref[...] + gate * dx[j * t:(j + 1) * t, :]


def _diff_readout(xs, n_ctx, ot, modv, subln, wo, lambda_init):
    n, n_lat = ot.shape
    d = xs.shape[1]
    t = ROW_TILE
    ctx_tiles = n_ctx // t
    per = READOUT_TILES
    assert n_lat % (per * t) == 0
    kern = functools.partial(_diff_readout_kernel, out_scale=1.0 - lambda_init)
    x_specs = [pl.BlockSpec((t, d), functools.partial(lambda i, j: (per * i + j + ctx_tiles, 0), j=j))
               for j in range(per)]
    return pl.pallas_call(
        kern,
        grid=(n_lat // (per * t),),
        in_specs=x_specs + [pl.BlockSpec((n, per * t), lambda i: (0, i)),
                            _const_spec(modv.shape), _const_spec((subln.shape[0], 1)),
                            _const_spec(wo.shape)],
        out_specs=pl.BlockSpec((per * t, d), lambda i: (i, 0)),
        out_shape=jax.ShapeDtypeStruct((n_lat, d), F32),
        compiler_params=_cparams("parallel"),
        name="diff_readout",
    )(*([xs] * per), ot, modv, subln.reshape(-1, 1), wo.astype(BF16))


def kernel(x, c, ctx, c_ctx, mod_w, mod_b, norm_mix, norm_ffn, gla_wq, gla_wk, gla_wv, gla_wr, gla_wg1, gla_wg2, gla_bg, gla_norm, gla_wo, diff_wq, diff_wk, diff_wv, diff_lq1, diff_lk1, diff_lq2, diff_lk2, diff_subln, diff_wo, ffn_wup, ffn_conv, ffn_conv_b, ffn_wdown, final_norm):
    bsz, n_lat, d = x.shape
    n_ctx = ctx.shape[1]
    depth = mod_w.shape[0]
    assert bsz == 1 and depth == 2, "layer 0 is the GLA mixer, layer 1 the differential attention"
    x2 = x.reshape(n_lat, d)
    ctx2 = ctx.reshape(n_ctx, d)
    cc = jnp.zeros((8, d), F32).at[0].set(c[0]).at[1].set(c_ctx)
    mod = _modulation(cc, mod_w, mod_b)

    q, k, v, gr, lg = _gla_proj(ctx2, x2, mod[0], norm_mix[0], gla_wq[0], gla_wk[0], gla_wv[0],
                                gla_wr[0], gla_wg1[0], gla_wg2[0], gla_bg[0])
    o_f, o_b = _gla_scan(q, k, v, lg, n_ctx)
    xs = _gla_readout(ctx2, x2, o_f, o_b, gr, mod[0], gla_norm[0], gla_wo[0])
    xs = _conv_ffn(xs, mod[0], norm_ffn[0], ffn_wup[0], ffn_conv[0], ffn_conv_b[0], ffn_wdown[0],
                   final_norm, ctx_tiles=n_ctx // ROW_TILE, final=False)

    lambda_init = 0.8 - 0.6 * math.exp(-0.3 * 1)
    qt, kk, vt = _diff_proj(xs, n_ctx, mod[1], norm_mix[1], diff_wq[0], diff_wk[0], diff_wv[0])
    lam_vecs = jnp.zeros((8, DIFF_HEAD_DIM), F32).at[0:4].set(
        jnp.concatenate([diff_lq1, diff_lk1, diff_lq2, diff_lk2], axis=0))
    ot = _diff_attn(qt, kk, vt, lam_vecs, lambda_init)
    xl = _diff_readout(xs, n_ctx, ot, mod[1], diff_subln[0], diff_wo[0], lambda_init)
    out = _conv_ffn(xl, mod[1], norm_ffn[1], ffn_wup[1], ffn_conv[1], ffn_conv_b[1], ffn_wdown[1],
                    final_norm, ctx_tiles=0, final=True)
    return out.reshape(bsz, n_lat, d)
```

```python
import functools
import math

import numpy as np
import jax
import jax.numpy as jnp
from jax import lax
from jax.experimental import pallas as pl
from jax.experimental.pallas import tpu as pltpu

GRID_W = 64
GLA_HEADS = 4
GLA_GATE_RANK = 16
GLA_GATE_NORM = 16.0
GLA_CHUNK = 64
DIFF_HEAD_DIM = 64
ROPE_BASE = 10000.0
CONV_WIDTH = 3
N_MOD = 6
EPS = 1e-6

LANES = 128
BF16_SUBLANES = 16
VMEM_LIMIT = 56 * 1024 * 1024

ROW_TILE = 256
FFN_COL_CHUNK = 256
ATT_TQ = 512
ATT_SUB = 512
ATT_TK = 3328
ATT_KW = 256
ATT_OVERFLOW_LOG2 = 64.0
MOD_COL_TILE = 1536

F32 = jnp.float32
BF16 = jnp.bfloat16

_NT = (((1,), (1,)), ((), ()))
_TN = (((0,), (0,)), ((), ()))


def _cparams(*sem):
    return pltpu.CompilerParams(dimension_semantics=sem, vmem_limit_bytes=VMEM_LIMIT)


def _const_spec(shape):
    nd = len(shape)
    return pl.BlockSpec(shape, lambda *_: (0,) * nd, pipeline_mode=pl.Buffered(1))


def _silu(x):
    return x * (1.0 / (1.0 + jnp.exp(-x)))


def _rms(x):
    return x * lax.rsqrt(jnp.mean(x * x, axis=-1, keepdims=True) + EPS)


def _mod_rows(modv_ref, is_ctx, k, d):
    sl = slice(k * d, (k + 1) * d)
    return jnp.where(is_ctx, modv_ref[1:2, sl], modv_ref[0:1, sl])


def _modnorm(x, g, scale, shift):
    return (_rms(x) * g) * (1.0 + scale) + shift


def _split2(x):
    hi = x.astype(BF16)
    lo = (x - hi.astype(F32)).astype(BF16)
    return hi, lo


def _mod_kernel(cc_ref, w_ref, b_ref, o_ref):
    s = _silu(cc_ref[...])
    w = w_ref[0]
    s_hi, s_lo = _split2(s)
    w_hi, w_lo = _split2(w)
    acc = jnp.dot(s_hi, w_hi, preferred_element_type=F32)
    acc += jnp.dot(s_lo, w_hi, preferred_element_type=F32)
    acc += jnp.dot(s_hi, w_lo, preferred_element_type=F32)
    o_ref[0] = acc + b_ref[0]


def _modulation(cc, mod_w, mod_b):
    depth, d, n = mod_w.shape
    tn = MOD_COL_TILE
    return pl.pallas_call(
        _mod_kernel,
        grid=(depth, n // tn),
        in_specs=[pl.BlockSpec((8, d), lambda l, j: (0, 0)),
                  pl.BlockSpec((1, d, tn), lambda l, j: (l, 0, j)),
                  pl.BlockSpec((1, 1, tn), lambda l, j: (l, 0, j))],
        out_specs=pl.BlockSpec((1, 8, tn), lambda l, j: (l, 0, j)),
        out_shape=jax.ShapeDtypeStruct((depth, 8, n), F32),
        compiler_params=_cparams("parallel", "parallel"),
        name="modulation",
    )(cc, mod_w, mod_b.reshape(depth, 1, n))


def _gla_proj_kernel(ctx_ref, x_ref, modv_ref, g_ref, w_ref, wg1_ref, wg2_ref, bg_ref,
                     q_ref, k_ref, v_ref, r_ref, lg_ref, *, dk_total, dv_total, q_scale):
    d = x_ref.shape[1]
    is_ctx = pl.program_id(0) == 0
    xt = jnp.where(is_ctx, ctx_ref[...], x_ref[...])
    h = _modnorm(xt, g_ref[...], _mod_rows(modv_ref, is_ctx, 1, d),
                 _mod_rows(modv_ref, is_ctx, 0, d)).astype(BF16)
    qkvr = jnp.dot(h, w_ref[...], preferred_element_type=F32)
    o1 = dk_total
    o2 = 2 * dk_total
    o3 = o2 + dv_total
    q_ref[...] = qkvr[:, :o1] * q_scale
    k_ref[...] = qkvr[:, o1:o2]
    v_ref[...] = qkvr[:, o2:o3].astype(BF16)
    r_ref[...] = _silu(qkvr[:, o3:]).astype(BF16)
    low = jnp.dot(h, wg1_ref[...], preferred_element_type=F32).astype(BF16)
    z = jnp.dot(low, wg2_ref[...], preferred_element_type=F32) + bg_ref[...]
    log_sig = jnp.minimum(z, 0.0) - jnp.log1p(jnp.exp(-jnp.abs(z)))
    lg_ref[...] = log_sig * (1.0 / GLA_GATE_NORM)


def _gla_proj(ctx2, x2, modv, norm_g, wq, wk, wv, wr, wg1, wg2, bg):
    n_ctx, d = ctx2.shape
    n_lat = x2.shape[0]
    t = ROW_TILE
    assert n_ctx == t and n_lat % t == 0
    n_all = n_ctx + n_lat
    dk_total = wq.shape[1]
    dv_total = wv.shape[1]
    rank = wg1.shape[2]
    w = jnp.concatenate([wq, wk, wv, wr], axis=1).astype(BF16)
    wg1c = jnp.zeros((d, LANES), F32).at[:, :rank].set(wg1[0]).at[:, rank:2 * rank].set(wg1[1])
    wg2c = (jnp.zeros((LANES, 2 * dk_total), F32)
            .at[:rank, :dk_total].set(wg2[0]).at[rank:2 * rank, dk_total:].set(wg2[1]))
    bgc = bg.reshape(1, 2 * dk_total)
    row = lambda i: (i, 0)
    kern = functools.partial(_gla_proj_kernel, dk_total=dk_total, dv_total=dv_total,
                             q_scale=(dk_total // GLA_HEADS) ** -0.5)
    return pl.pallas_call(
        kern,
        grid=(n_all // t,),
        in_specs=[pl.BlockSpec((t, d), lambda i: (0, 0)),
                  pl.BlockSpec((t, d), lambda i: (jnp.maximum(i - 1, 0), 0)),
                  _const_spec(modv.shape), _const_spec((1, d)),
                  _const_spec(w.shape), _const_spec(wg1c.shape), _const_spec(wg2c.shape),
                  _const_spec(bgc.shape)],
        out_specs=[pl.BlockSpec((t, dk_total), row), pl.BlockSpec((t, dk_total), row),
                   pl.BlockSpec((t, dv_total), row), pl.BlockSpec((t, dv_total), row),
                   pl.BlockSpec((t, 2 * dk_total), row)],
        out_shape=[jax.ShapeDtypeStruct((n_all, dk_total), F32),
                   jax.ShapeDtypeStruct((n_all, dk_total), F32),
                   jax.ShapeDtypeStruct((n_all, dv_total), BF16),
                   jax.ShapeDtypeStruct((n_all, dv_total), BF16),
                   jax.ShapeDtypeStruct((n_all, 2 * dk_total), F32)],
        compiler_params=_cparams("parallel"),
        name="gla_proj",
    )(ctx2, x2, modv, norm_g.reshape(1, d), w, wg1c.astype(BF16), wg2c.astype(BF16), bgc)


N_LEVELS = int(math.log2(GLA_CHUNK))
N_DECAY_BLOCKS = 2 + N_LEVELS


def _scan_tables():
    c = GLA_CHUNK
    t = np.arange(c)[:, None]
    u = np.arange(c)[None, :]
    blocks = [(u <= t), (u > t)]
    masks = [(t == u)]
    for lvl in range(1, N_LEVELS + 1):
        size = 1 << lvl
        half = size // 2
        start = (t // size) * size
        mid = start + half - 1
        upper = (t - start) >= half
        blocks.append((upper & (u > mid) & (u <= t)) | ((~upper) & (u > t) & (u <= mid)))
        same = (t // size) == (u // size)
        masks.append(same & upper & ((u - (u // size) * size) < half))
    fwd = np.concatenate(blocks, axis=0).astype(np.float32)
    fmask = np.concatenate(masks, axis=0).astype(np.float32)
    rev = lambda m: m.reshape(-1, c, c)[:, ::-1, ::-1].reshape(-1, c)
    twice = lambda m: np.concatenate([m, m], axis=1)
    return (np.stack([twice(fwd), twice(rev(fwd))]), np.stack([fmask, rev(fmask)]))


def _gla_scan_kernel(dmat_ref, mask_ref, qf, kf, vf, gf, qb, kb, vb, gb, of_ref, ob_ref, st_ref):
    @pl.when(pl.program_id(0) == 0)
    def _():
        st_ref[...] = jnp.zeros_like(st_ref)

    c = GLA_CHUNK
    n_sub = qf.shape[0] // c
    dk = qf.shape[1] // GLA_HEADS
    dv = vf.shape[1] // GLA_HEADS
    heads = range(GLA_HEADS)
    hsl = [slice(h * dk, (h + 1) * dk) for h in heads]
    vsl = [slice(h * dv, (h + 1) * dv) for h in heads]
    refs = ((qf, kf, vf, gf, of_ref), (qb, kb, vb, gb, ob_ref))
    work = [(d, ci) for step in range(n_sub) for d, ci in ((0, step), (1, n_sub - 1 - step))]
    rows = lambda ci: slice(ci * c, (ci + 1) * c)

    decay_x = {}
    for d, ci in work:
        g_hi, g_lo = _split2(refs[d][3][rows(ci), :])
        e = jnp.dot(dmat_ref[d], jnp.concatenate([g_hi, g_lo], axis=0),
                    preferred_element_type=F32)
        decay_x[d, ci] = jnp.exp(e)
    blk = lambda d, ci, n, h: decay_x[d, ci][n * c:(n + 1) * c, hsl[h]]

    scores = {}
    for d, ci in work:
        q_all = refs[d][0][rows(ci), :]
        k_all = refs[d][1][rows(ci), :]
        for h in heads:
            q = q_all[:, hsl[h]]
            k = k_all[:, hsl[h]]
            parts = [lax.dot_general(q.astype(BF16), k.astype(BF16), _NT,
                                     preferred_element_type=F32)]
            for lvl in range(1, N_LEVELS + 1):
                f = blk(d, ci, 1 + lvl, h)
                parts.append(lax.dot_general((q * f).astype(BF16), (k * f).astype(BF16), _NT,
                                             preferred_element_type=F32))
            scores[d, ci, h] = parts

    increments = {}
    intra = {}
    for d, ci in work:
        k_all = refs[d][1][rows(ci), :]
        v_all = refs[d][2][rows(ci), :]
        for h in heads:
            a = mask_ref[d, 0:c, :] * scores[d, ci, h][0]
            for lvl in range(1, N_LEVELS + 1):
                a += mask_ref[d, lvl * c:(lvl + 1) * c, :] * scores[d, ci, h][lvl]
            intra[d, ci, h] = jnp.dot(a.astype(BF16), v_all[:, vsl[h]], preferred_element_type=F32)
            k_end = (k_all[:, hsl[h]] * blk(d, ci, 1, h)).astype(BF16)
            increments[d, ci, h] = lax.dot_general(v_all[:, vsl[h]], k_end, _TN,
                                                   preferred_element_type=F32)

    for d, ci in work:
        last = c - 1 if d == 0 else 0
        q_all = refs[d][0][rows(ci), :]
        outs = []
        for h in heads:
            st = st_ref[d, h]
            inter = lax.dot_general((q_all[:, hsl[h]] * blk(d, ci, 0, h)).astype(BF16),
                                    st.astype(BF16), _NT, preferred_element_type=F32)
            outs.append(intra[d, ci, h] + inter)
            decay = decay_x[d, ci][last:last + 1, hsl[h]]
            st_ref[d, h] = st * decay + increments[d, ci, h]
        refs[d][4][rows(ci), :] = jnp.concatenate(outs, axis=1).astype(refs[d][4].dtype)


def _gla_scan(q, k, v, lg, n_ctx):
    n_all, dk_total = q.shape
    dv_total = v.shape[1]
    t = ROW_TILE
    assert t % GLA_CHUNK == 0 and n_ctx == t
    n_tiles = n_all // t
    dmat, masks = _scan_tables()
    fwd = lambda i: (i, 0)
    bidx = lambda i: jnp.where(i == 0, 0, n_tiles - i)
    bwd = lambda i: (bidx(i), 0)
    bwd_g = lambda i: (bidx(i), 1)
    return pl.pallas_call(
        _gla_scan_kernel,
        grid=(n_tiles,),
        in_specs=[_const_spec(dmat.shape), _const_spec(masks.shape),
                  pl.BlockSpec((t, dk_total), fwd), pl.BlockSpec((t, dk_total), fwd),
                  pl.BlockSpec((t, dv_total), fwd), pl.BlockSpec((t, dk_total), fwd),
                  pl.BlockSpec((t, dk_total), bwd), pl.BlockSpec((t, dk_total), bwd),
                  pl.BlockSpec((t, dv_total), bwd), pl.BlockSpec((t, dk_total), bwd_g)],
        out_specs=[pl.BlockSpec((t, dv_total), fwd), pl.BlockSpec((t, dv_total), bwd)],
        out_shape=[jax.ShapeDtypeStruct((n_all, dv_total), BF16),
                   jax.ShapeDtypeStruct((n_all, dv_total), BF16)],
        scratch_shapes=[pltpu.VMEM((2, GLA_HEADS, dv_total // GLA_HEADS, dk_total // GLA_HEADS), F32)],
        compiler_params=_cparams("arbitrary"),
        name="gla_scan",
    )(jnp.asarray(dmat, BF16), jnp.asarray(masks, F32), q, k, v, lg, q, k, v, lg)


def _gla_readout_kernel(ctx_ref, x_ref, of_ref, ob_ref, r_ref, modv_ref, gn_ref, wo_ref, out_ref):
    d = x_ref.shape[1]
    dv = gn_ref.shape[1]
    is_ctx = pl.program_id(0) == 0
    xt = jnp.where(is_ctx, ctx_ref[...], x_ref[...])
    o = of_ref[...].astype(F32) + ob_ref[...].astype(F32)
    parts = []
    for h in range(GLA_HEADS):
        parts.append(_rms(o[:, h * dv:(h + 1) * dv]) * gn_ref[...])
    y = (jnp.concatenate(parts, axis=1) * r_ref[...].astype(F32)).astype(BF16)
    dx = jnp.dot(y, wo_ref[...], preferred_element_type=F32)
    out_ref[...] = xt + _mod_rows(modv_ref, is_ctx, 2, d) * dx


def _gla_readout(ctx2, x2, o_f, o_b, gr, modv, gn, wo):
    n_ctx, d = ctx2.shape
    n_all, dv_total = o_f.shape
    t = ROW_TILE
    row = lambda i: (i, 0)
    return pl.pallas_call(
        _gla_readout_kernel,
        grid=(n_all // t,),
        in_specs=[pl.BlockSpec((t, d), lambda i: (0, 0)),
                  pl.BlockSpec((t, d), lambda i: (jnp.maximum(i - 1, 0), 0)),
                  pl.BlockSpec((t, dv_total), row), pl.BlockSpec((t, dv_total), row),
                  pl.BlockSpec((t, dv_total), row),
                  _const_spec(modv.shape), _const_spec((1, gn.shape[0])), _const_spec(wo.shape)],
        out_specs=pl.BlockSpec((t, d), row),
        out_shape=jax.ShapeDtypeStruct((n_all, d), F32),
        compiler_params=_cparams("parallel"),
        name="gla_readout",
    )(ctx2, x2, o_f, o_b, gr, modv, gn.reshape(1, -1), wo.astype(BF16))


HALO = BF16_SUBLANES


def _ffn_subtile(sub, n_sub, n_tiles, ctx_tiles, final, prev_ref, x_ref, next_ref, modv_ref, g_ref,
                 wup_ref, cw_ref, cb_ref, wdn_ref, fin_ref, out_ref, lhs_ref, u_ref, act_ref):
    t = ROW_TILE
    d = x_ref.shape[1]
    d_ff = wdn_ref.shape[0]
    s = pl.program_id(0) * n_sub + sub
    is_ctx = s < ctx_tiles
    has_prev = jnp.logical_and(s != 0, s != ctx_tiles)
    has_next = jnp.logical_and(s != n_tiles - 1, s != ctx_tiles - 1)
    r0 = sub * t
    prev = prev_ref[...] if sub == 0 else x_ref[r0 - HALO:r0, :]
    nxt = next_ref[...] if sub == n_sub - 1 else x_ref[r0 + t:r0 + t + HALO, :]
    g = g_ref[...]
    scale = _mod_rows(modv_ref, is_ctx, 4, d)
    shift = _mod_rows(modv_ref, is_ctx, 3, d)
    xt = x_ref[r0:r0 + t, :]
    lhs_ref[sub, 0:HALO, :] = jnp.where(has_prev, _modnorm(prev, g, scale, shift), 0.0).astype(BF16)
    lhs_ref[sub, HALO:HALO + t, :] = _modnorm(xt, g, scale, shift).astype(BF16)
    lhs_ref[sub, HALO + t:, :] = jnp.where(has_next, _modnorm(nxt, g, scale, shift), 0.0).astype(BF16)
    yield
    lhs = lhs_ref[sub]
    cc = FFN_COL_CHUNK
    for c in range(d_ff // cc):
        halves = []
        for half, off in ((0, c * cc), (1, d_ff + c * cc)):
            cols = slice(off, off + cc)
            u_ref[sub, half, c] = jnp.dot(lhs, wup_ref[:, cols], preferred_element_type=F32)
            u = u_ref[sub, half, c]
            u_prev = pltpu.roll(u, 1, 0)[HALO:HALO + t, :]
            u_next = pltpu.roll(u, u.shape[0] - 1, 0)[HALO:HALO + t, :]
            halves.append(u_prev * cw_ref[0:1, cols]
                          + u[HALO:HALO + t, :] * cw_ref[1:2, cols]
                          + u_next * cw_ref[2:3, cols]
                          + cb_ref[:, cols])
        act_ref[sub, :, c * cc:(c + 1) * cc] = (_silu(halves[0]) * halves[1]).astype(BF16)
    yield
    down = jnp.dot(act_ref[sub], wdn_ref[...], preferred_element_type=F32)
    y = xt + _mod_rows(modv_ref, is_ctx, 5, d) * down
    if final:
        y = _rms(y) * fin_ref[...]
    out_ref[r0:r0 + t, :] = y


def _ffn_kernel(*refs, n_sub, n_tiles, ctx_tiles, final):
    tiles = [_ffn_subtile(sub, n_sub, n_tiles, ctx_tiles, final, *refs) for sub in range(n_sub)]
    for _phase in range(3):
        for tile in tiles:
            next(tile, None)


def _conv_ffn(xs, modv, norm_g, w_up, w_conv, b_conv, w_down, final_g, *, ctx_tiles, final):
    n, d = xs.shape
    d_ff = w_down.shape[0]
    t = ROW_TILE
    assert d_ff % FFN_COL_CHUNK == 0
    n_tiles = n // t
    n_sub = 2 if n_tiles % 2 == 0 else 1
    per = n_sub * t // HALO
    n_halo_blocks = n // HALO
    kern = functools.partial(_ffn_kernel, n_sub=n_sub, n_tiles=n_tiles, ctx_tiles=ctx_tiles,
                             final=final)
    return pl.pallas_call(
        kern,
        grid=(n_tiles // n_sub,),
        in_specs=[pl.BlockSpec((HALO, d), lambda i: (jnp.maximum(i * per - 1, 0), 0)),
                  pl.BlockSpec((n_sub * t, d), lambda i: (i, 0)),
                  pl.BlockSpec((HALO, d), lambda i: (jnp.minimum((i + 1) * per, n_halo_blocks - 1), 0)),
                  _const_spec(modv.shape), _const_spec((1, d)),
                  _const_spec(w_up.shape), _const_spec(w_conv.shape), _const_spec((1, 2 * d_ff)),
                  _const_spec(w_down.shape), _const_spec((1, d))],
        out_specs=pl.BlockSpec((n_sub * t, d), lambda i: (i, 0)),
        out_shape=jax.ShapeDtypeStruct((n, d), F32),
        scratch_shapes=[pltpu.VMEM((n_sub, t + 2 * HALO, d), BF16),
                        pltpu.VMEM((n_sub, 2, d_ff // FFN_COL_CHUNK, t + 2 * HALO, FFN_COL_CHUNK), F32),
                        pltpu.VMEM((n_sub, t, d_ff), BF16)],
        compiler_params=_cparams("parallel"),
        name="conv_ffn_final" if final else "conv_ffn",
    )(xs, xs, xs, modv, norm_g.reshape(1, d), w_up.astype(BF16), w_conv,
      b_conv.reshape(1, -1), w_down.astype(BF16), final_g.reshape(1, d))


def _rope_tables(n_tokens, width):
    rows = n_tokens // GRID_W
    row = np.repeat(np.arange(rows, dtype=np.float32), GRID_W)
    col = np.tile(np.arange(GRID_W, dtype=np.float32), rows)
    quarter = DIFF_HEAD_DIM // 4
    inv = np.float32(ROPE_BASE) ** (-np.arange(quarter, dtype=np.float32) / np.float32(quarter))
    ang_r = row[:, None] * inv
    ang_c = col[:, None] * inv
    ang = np.concatenate([ang_r, ang_r, ang_c, ang_c], axis=-1).astype(np.float64)
    sign = np.tile(np.concatenate([-np.ones(quarter), np.ones(quarter)]), 2)
    reps = width // DIFF_HEAD_DIM
    return (np.tile(np.cos(ang), (1, reps)).astype(np.float32),
            np.tile(np.sin(ang) * sign, (1, reps)).astype(np.float32))


def _diff_proj_kernel(x_ref, cos_ref, sin_ref, modv_ref, g_ref, w_ref, qt_ref, k_ref, vt_ref):
    t, d = x_ref.shape
    is_ctx = pl.program_id(0) == 0
    h = _modnorm(x_ref[...], g_ref[...], _mod_rows(modv_ref, is_ctx, 1, d),
                 _mod_rows(modv_ref, is_ctx, 0, d)).astype(BF16)
    n = w_ref.shape[1] // 3
    quarter = DIFF_HEAD_DIM // 4
    reps = n // LANES
    cos = jnp.concatenate([cos_ref[...]] * reps, axis=1)
    sin = jnp.concatenate([sin_ref[...]] * reps, axis=1)
    lane = lax.broadcasted_iota(jnp.int32, (t, n), 1)
    first_half = (lane % (2 * quarter)) < quarter

    def rope(a):
        partner = jnp.where(first_half, pltpu.roll(a, n - quarter, 1), pltpu.roll(a, quarter, 1))
        return jnp.where(is_ctx, a, a * cos + partner * sin)

    qkv = jnp.dot(h, w_ref[...], preferred_element_type=F32)
    q = rope(qkv[:, :n]) * (DIFF_HEAD_DIM ** -0.5 * math.log2(math.e))
    qt_ref[...] = q.T.astype(BF16)
    k = rope(qkv[:, n:2 * n])
    hw = 2 * DIFF_HEAD_DIM
    ones_col = jnp.where(lax.broadcasted_iota(jnp.int32, (t, ATT_KW - hw), 1) < SHIFT_PARTS, 1.0, 0.0)
    parts = []
    for hh in range(n // hw):
        parts += [k[:, hh * hw:(hh + 1) * hw], ones_col]
    k_ref[...] = jnp.concatenate(parts, axis=1).astype(BF16)
    vt_ref[...] = qkv[:, 2 * n:].T.astype(BF16)


def _diff_proj(xs, n_ctx, modv, norm_g, wq, wk, wv):
    n_all, d = xs.shape
    n_lat = n_all - n_ctx
    t = ROW_TILE
    n = wq.shape[1]
    w = jnp.concatenate([wq, wk, wv], axis=1).astype(BF16)
    cos, sin = _rope_tables(n_lat, LANES)
    kaug = n // (2 * DIFF_HEAD_DIM) * ATT_KW
    lat = lambda i: (jnp.maximum(i - 1, 0), 0)
    return pl.pallas_call(
        _diff_proj_kernel,
        grid=(n_all // t,),
        in_specs=[pl.BlockSpec((t, d), lambda i: (i, 0)),
                  pl.BlockSpec((t, LANES), lat), pl.BlockSpec((t, LANES), lat),
                  _const_spec(modv.shape), _const_spec((1, d)), _const_spec(w.shape)],
        out_specs=[pl.BlockSpec((n, t), lambda i: (0, jnp.maximum(i - 1, 0))),
                   pl.BlockSpec((t, kaug), lambda i: (i, 0)),
                   pl.BlockSpec((n, t), lambda i: (0, i))],
        out_shape=[jax.ShapeDtypeStruct((n, n_lat), BF16),
                   jax.ShapeDtypeStruct((n_all, kaug), BF16),
                   jax.ShapeDtypeStruct((n, n_all), BF16)],
        compiler_params=_cparams("arbitrary"),
        name="diff_proj",
    )(xs, cos, sin, modv, norm_g.reshape(1, d), w)


SHIFT_ROWS = BF16_SUBLANES
SHIFT_PARTS = 2


def _diff_attn_kernel(lam_ref, qt_ref, k_ref, vt_ref, o_ref, qa_ref, l_ref, acc_ref,
                      *, lambda_init):
    hd = DIFF_HEAD_DIM
    hw = 2 * hd
    tq = qt_ref.shape[1]
    kw = k_ref.shape[1]
    n_blocks = k_ref.shape[0] // ATT_TK
    tk = ATT_TK
    qt = qt_ref[...]
    rows = lax.broadcasted_iota(jnp.int32, qt.shape, 0)
    zero = jnp.zeros_like(qt)
    shift_row = lax.broadcasted_iota(jnp.int32, (SHIFT_ROWS, tq), 0)

    def block(j, br):
        start = j * tk if isinstance(j, int) else pl.multiple_of(j * tk, LANES)
        kb = k_ref[pl.ds(start, tk), :]
        vb = vt_ref[:, pl.ds(start, tk)]
        bm, cs, pv = [], [], []
        for sub in range(tq // ATT_SUB):
            cols = slice(sub * ATT_SUB, (sub + 1) * ATT_SUB)
            s = jnp.dot(kb, qa_ref[br, :, cols], preferred_element_type=F32)
            bm.append(jnp.max(s, axis=0, keepdims=True))
            p = jnp.exp2(s)
            cs.append(jnp.sum(p, axis=0, keepdims=True))
            pv.append(jnp.dot(vb, p.astype(BF16), preferred_element_type=F32))
        return (jnp.concatenate(bm, axis=1), jnp.concatenate(cs, axis=1),
                jnp.concatenate(pv, axis=1))

    for br in range(2):
        keep = jnp.logical_and(rows >= br * hd, rows < (br + 1) * hd)
        qa_ref[br, 0:hw, :] = jnp.where(keep, qt, zero)
        qa_ref[br, hw:, :] = jnp.zeros((kw - hw, tq), BF16)

    top = [None, None]
    for j in range(n_blocks):
        for br in range(2):
            bm, cs, pv = block(j, br)
            if j == 0:
                top[br] = bm
                l_ref[br] = cs
                acc_ref[br] = pv
            else:
                top[br] = jnp.maximum(top[br], bm)
                l_ref[br] += cs
                acc_ref[br] += pv

    @pl.when(jnp.max(jnp.maximum(jnp.abs(top[0]), jnp.abs(top[1]))) > ATT_OVERFLOW_LOG2)
    def _():
        for br in range(2):
            hi, lo = _split2(top[br])
            qa_ref[br, hw:hw + SHIFT_ROWS, :] = jnp.where(
                shift_row == 0, -hi.astype(F32),
                jnp.where(shift_row == 1, -lo.astype(F32), 0.0)).astype(BF16)
        l_ref[...] = jnp.zeros_like(l_ref)
        acc_ref[...] = jnp.zeros_like(acc_ref)

        def sum_body(j, carry):
            for br in range(2):
                _, cs, pv = block(j, br)
                l_ref[br] += cs
                acc_ref[br] += pv
            return carry

        lax.fori_loop(0, n_blocks, sum_body, 0)

    lam = (jnp.exp(jnp.sum(lam_ref[0:1, :] * lam_ref[1:2, :], axis=1, keepdims=True))
           - jnp.exp(jnp.sum(lam_ref[2:3, :] * lam_ref[3:4, :], axis=1, keepdims=True))
           + lambda_init)
    o_ref[...] = (acc_ref[0] / l_ref[0] - lam * (acc_ref[1] / l_ref[1])).astype(o_ref.dtype)


def _diff_attn(qt, k, vt, lam_vecs, lambda_init):
    n, n_lat = qt.shape
    n_all = k.shape[0]
    hw = 2 * DIFF_HEAD_DIM
    heads = n // hw
    tq = ATT_TQ
    assert n_lat % tq == 0 and n_all % ATT_TK == 0 and tq % ATT_SUB == 0
    kern = functools.partial(_diff_attn_kernel, lambda_init=lambda_init)
    return pl.pallas_call(
        kern,
        grid=(heads, n_lat // tq),
        in_specs=[pl.BlockSpec(lam_vecs.shape, lambda h, i: (0, 0)),
                  pl.BlockSpec((hw, tq), lambda h, i: (h, i)),
                  pl.BlockSpec((n_all, ATT_KW), lambda h, i: (0, h)),
                  pl.BlockSpec((hw, n_all), lambda h, i: (h, 0))],
        out_specs=pl.BlockSpec((hw, tq), lambda h, i: (h, i)),
        out_shape=jax.ShapeDtypeStruct((n, n_lat), BF16),
        scratch_shapes=[pltpu.VMEM((2, ATT_KW, tq), BF16), pltpu.VMEM((2, 1, tq), F32),
                        pltpu.VMEM((2, hw, tq), F32)],
        compiler_params=_cparams("arbitrary", "arbitrary"),
        name="diff_attn",
    )(lam_vecs, qt, k, vt)


READOUT_TILES = 2


def _diff_readout_kernel(*refs, out_scale):
    x_refs = refs[:READOUT_TILES]
    ot_ref, modv_ref, sub_ref, wo_ref, out_ref = refs[READOUT_TILES:]
    t, d = x_refs[0].shape
    hw = sub_ref.shape[0]
    ot = ot_ref[...].astype(F32)
    parts = []
    for h in range(ot.shape[0] // hw):
        oh = ot[h * hw:(h + 1) * hw, :]
        ms = jnp.mean(oh * oh, axis=0, keepdims=True)
        parts.append(oh * lax.rsqrt(ms + EPS) * sub_ref[...] * out_scale)
    y = jnp.concatenate(parts, axis=0).T.astype(BF16)
    dx = jnp.dot(y, wo_ref[...], preferred_element_type=F32)
    gate = modv_ref[0:1, 2 * d:3 * d]
    for j, x_ref in enumerate(x_refs):
        out_ref[j * t:(j + 1) * t, :] = x_ref[...] + gate * dx[j * t:(j + 1) * t, :]


def _diff_readout(xs, n_ctx, ot, modv, subln, wo, lambda_init):
    n, n_lat = ot.shape
    d = xs.shape[1]
    t = ROW_TILE
    ctx_tiles = n_ctx // t
    per = READOUT_TILES
    assert n_lat % (per * t) == 0
    kern = functools.partial(_diff_readout_kernel, out_scale=1.0 - lambda_init)
    x_specs = [pl.BlockSpec((t, d), functools.partial(lambda i, j: (per * i + j + ctx_tiles, 0), j=j))
               for j in range(per)]
    return pl.pallas_call(
        kern,
        grid=(n_lat // (per * t),),
        in_specs=x_specs + [pl.BlockSpec((n, per * t), lambda i: (0, i)),
                            _const_spec(modv.shape), _const_spec((subln.shape[0], 1)),
                            _const_spec(wo.shape)],
        out_specs=pl.BlockSpec((per * t, d), lambda i: (i, 0)),
        out_shape=jax.ShapeDtypeStruct((n_lat, d), F32),
        compiler_params=_cparams("parallel"),
        name="diff_readout",
    )(*([xs] * per), ot, modv, subln.reshape(-1, 1), wo.astype(BF16))


def kernel(x, c, ctx, c_ctx, mod_w, mod_b, norm_mix, norm_ffn, gla_wq, gla_wk, gla_wv, gla_wr, gla_wg1, gla_wg2, gla_bg, gla_norm, gla_wo, diff_wq, diff_wk, diff_wv, diff_lq1, diff_lk1, diff_lq2, diff_lk2, diff_subln, diff_wo, ffn_wup, ffn_conv, ffn_conv_b, ffn_wdown, final_norm):
    bsz, n_lat, d = x.shape
    n_ctx = ctx.shape[1]
    depth = mod_w.shape[0]
    assert bsz == 1 and depth == 2, "layer 0 is the GLA mixer, layer 1 the differential attention"
    x2 = x.reshape(n_lat, d)
    ctx2 = ctx.reshape(n_ctx, d)
    cc = jnp.zeros((8, d), F32).at[0].set(c[0]).at[1].set(c_ctx)
    mod = _modulation(cc, mod_w, mod_b)

    q, k, v, gr, lg = _gla_proj(ctx2, x2, mod[0], norm_mix[0], gla_wq[0], gla_wk[0], gla_wv[0],
                                gla_wr[0], gla_wg1[0], gla_wg2[0], gla_bg[0])
    o_f, o_b = _gla_scan(q, k, v, lg, n_ctx)
    xs = _gla_readout(ctx2, x2, o_f, o_b, gr, mod[0], gla_norm[0], gla_wo[0])
    xs = _conv_ffn(xs, mod[0], norm_ffn[0], ffn_wup[0], ffn_conv[0], ffn_conv_b[0], ffn_wdown[0],
                   final_norm, ctx_tiles=n_ctx // ROW_TILE, final=False)

    lambda_init = 0.8 - 0.6 * math.exp(-0.3 * 1)
    qt, kk, vt = _diff_proj(xs, n_ctx, mod[1], norm_mix[1], diff_wq[0], diff_wk[0], diff_wv[0])
    lam_vecs = jnp.zeros((8, DIFF_HEAD_DIM), F32).at[0:4].set(
        jnp.concatenate([diff_lq1, diff_lk1, diff_lq2, diff_lk2], axis=0))
    ot = _diff_attn(qt, kk, vt, lam_vecs, lambda_init)
    xl = _diff_readout(xs, n_ctx, ot, mod[1], diff_subln[0], diff_wo[0], lambda_init)
    out = _conv_ffn(xl, mod[1], norm_ffn[1], ffn_wup[1], ffn_conv[1], ffn_conv_b[1], ffn_wdown[1],
                    final_norm, ctx_tiles=0, final=True)
    return out.reshape(bsz, n_lat, d)
```
